```python
import math
import jax, jax.numpy as jnp
from jax import lax
import numpy as np

D_MODEL = 1024
BATCH = 4
SEQ = 4096
DEPTH = 1
DEC_BATCH = 32
DEC_SEQ = 4
PAST_LEN = 8192
PAGE_SIZE = 128

D_CONV = D_MODEL // 2
CONV_W = 31
N_HEADS = 16
HEAD_DIM = D_MODEL // N_HEADS
N_KV_HEADS = 4
KV_GROUP = N_HEADS // N_KV_HEADS
IDX_HEADS = 8
IDX_DIM = 64
TOPK_MAX = 256
Q_BLOCK = 128
ROPE_THETA = 500000.0
ROPE_FRAC = 4
PEER_HEADS = 8
PEER_NKEYS = 128
PEER_N_EXPERTS = PEER_NKEYS * PEER_NKEYS
PEER_DQ = 128
PEER_TOPK = 16
PEER_BLOCK = 128
ALPHA = (2.0 * DEPTH) ** 0.25
BETA = (8.0 * DEPTH) ** -0.25
LN_EPS = 1e-5

kernel_name = 'gated_conformer_dsa_peer_decoder_step'


def _proj_layout():
    sizes = (2 * D_CONV, N_HEADS * HEAD_DIM, N_KV_HEADS * HEAD_DIM, N_KV_HEADS * HEAD_DIM,
             IDX_HEADS * IDX_DIM, IDX_DIM, IDX_HEADS, D_MODEL, D_MODEL)
    pts, acc = [], 0
    for s in sizes[:-1]:
        acc += s
        pts.append(acc)
    return sizes, pts


def layer_norm(x, g, b):
    xf = x.astype(jnp.float32)
    mu = jnp.mean(xf, axis=-1, keepdims=True)
    var = jnp.mean(jnp.square(xf - mu), axis=-1, keepdims=True)
    return ((xf - mu) * lax.rsqrt(var + LN_EPS) * g + b).astype(x.dtype)


def partial_rope(x, pos):
    d = x.shape[-1]
    r = d // ROPE_FRAC
    half = r // 2
    inv = jnp.power(ROPE_THETA, -jnp.arange(half, dtype=jnp.float32) / half)
    ang = pos.astype(jnp.float32)[:, None] * inv
    cos = jnp.cos(ang)[:, None, :]
    sin = jnp.sin(ang)[:, None, :]
    x1 = x[..., :half].astype(jnp.float32)
    x2 = x[..., half:r].astype(jnp.float32)
    rot = jnp.concatenate([x1 * cos - x2 * sin, x2 * cos + x1 * sin], axis=-1)
    return jnp.concatenate([rot.astype(x.dtype), x[..., r:]], axis=-1)


def glu(z):
    a, b = jnp.split(z, 2, axis=-1)
    return a * jax.nn.sigmoid(b)


def project_inputs(x, pos, w_in):
    N, T, _ = x.shape
    _, pts = _proj_layout()
    z = jnp.einsum('ntd,dc->ntc', x, w_in)
    glu_in, q, k, v, qi, ki, wi, g_conv, g_attn = jnp.split(z, pts, axis=-1)
    u = glu(glu_in)
    q = partial_rope(q.reshape(N, T, N_HEADS, HEAD_DIM), pos)
    k = partial_rope(k.reshape(N, T, N_KV_HEADS, HEAD_DIM), pos)
    v = v.reshape(N, T, N_KV_HEADS, HEAD_DIM)
    qi = partial_rope(qi.reshape(N, T, IDX_HEADS, IDX_DIM), pos)
    ki = partial_rope(ki[:, :, None, :], pos)[:, :, 0, :]
    return u, q, k, v, qi, ki, wi, g_conv, g_attn


def conformer_conv(u, prev, conv_w, conv_b, conv_ln_g, conv_ln_b, w_conv_out):
    full = jnp.concatenate([prev, u], axis=1)
    y = lax.conv_general_dilated(full, conv_w[:, None, :], window_strides=(1,), padding='VALID',
                                 dimension_numbers=('NWC', 'WIO', 'NWC'),
                                 feature_group_count=D_CONV) + conv_b
    y = jax.nn.silu(layer_norm(y, conv_ln_g, conv_ln_b))
    out = jnp.einsum('ntc,cd->ntd', y, w_conv_out)
    return out, full[:, -(CONV_W - 1):]


def indexer_scores(qi, wi, ki):
    s = jax.nn.relu(jnp.einsum('nqhd,nld->nqhl', qi, ki).astype(jnp.float32))
    w = wi.astype(jnp.float32) * (IDX_HEADS * IDX_DIM) ** -0.5
    return jnp.einsum('nqh,nqhl->nql', w, s)


def select_keys(score, qpos, topk):
    kpos = jnp.arange(score.shape[-1], dtype=jnp.int32)
    adm = kpos[None, None, :] <= qpos[None, :, None]
    top_s, idx = lax.top_k(jnp.where(adm, score, -jnp.inf), topk)
    return idx, jnp.isfinite(top_s)


def sparse_attend(q, ks, vs, valid):
    N, tq = q.shape[:2]
    qg = q.reshape(N, tq, N_KV_HEADS, KV_GROUP, HEAD_DIM)
    s = jnp.einsum('nqhgd,nqkhd->nqhgk', qg, ks).astype(jnp.float32) * HEAD_DIM ** -0.5
    s = jnp.where(valid[:, :, None, None, :], s, -jnp.inf)
    p = jax.nn.softmax(s, axis=-1)
    o = jnp.einsum('nqhgk,nqkhd->nqhgd', p.astype(vs.dtype), vs)
    return o.reshape(N, tq, N_HEADS * HEAD_DIM)


_gather_rows = jax.vmap(lambda a, i: jnp.take(a, i, axis=0))


def attn_prompt(q, k, v, qi, ki, wi):
    N, T = q.shape[:2]
    topk = min(TOPK_MAX, T // 4)
    nb = T // Q_BLOCK
    pos = jnp.arange(T, dtype=jnp.int32)

    def blocks(a):
        return a.reshape((N, nb, Q_BLOCK) + a.shape[2:]).swapaxes(0, 1)

    def body(xs):
        qb, qib, wib, pb = xs
        idx, valid = select_keys(indexer_scores(qib, wib, ki), pb, topk)
        return sparse_attend(qb, _gather_rows(k, idx), _gather_rows(v, idx), valid)

    o = lax.map(body, (blocks(q), blocks(qi), blocks(wi), pos.reshape(nb, Q_BLOCK)))
    return o.swapaxes(0, 1).reshape(N, T, N_HEADS * HEAD_DIM)


def attn_sample(q, k_new, v_new, qi, ki_new, wi, cache_k, cache_v, cache_kidx, page_table):
    N, tq = q.shape[:2]
    past = page_table.shape[1] * PAGE_SIZE
    L = past + tq
    topk = min(TOPK_MAX, L // 4)
    ki_past = cache_kidx[page_table].reshape(N, past, IDX_DIM)
    ki_all = jnp.concatenate([ki_past, ki_new.astype(ki_past.dtype)], axis=1)
    qpos = past + jnp.arange(tq, dtype=jnp.int32)
    idx, valid = select_keys(indexer_scores(qi, wi, ki_all), qpos, topk)
    in_past = (idx < past)[..., None, None]
    pidx = jnp.minimum(idx, past - 1)
    n_idx = jnp.arange(N, dtype=jnp.int32)[:, None, None]
    phys = page_table[n_idx, pidx // PAGE_SIZE]
    slot = pidx % PAGE_SIZE
    nidx = jnp.clip(idx - past, 0, tq - 1)
    ks = jnp.where(in_past, cache_k[phys, slot], _gather_rows(k_new, nidx))
    vs = jnp.where(in_past, cache_v[phys, slot], _gather_rows(v_new, nidx))
    return sparse_attend(q, ks, vs, valid)


def peer_ffn(x2, peer_w_query, peer_keys1, peer_keys2, peer_u, peer_v):
    n = x2.shape[0]
    qh = jnp.einsum('nd,dhk->nhk', x2, peer_w_query)
    q1, q2 = qh[..., :PEER_DQ // 2], qh[..., PEER_DQ // 2:]
    s1 = jnp.einsum('nhk,hmk->nhm', q1, peer_keys1).astype(jnp.float32)
    s2 = jnp.einsum('nhk,hmk->nhm', q2, peer_keys2).astype(jnp.float32)
    t1, i1 = lax.top_k(s1, PEER_TOPK)
    t2, i2 = lax.top_k(s2, PEER_TOPK)
    cand = (t1[..., :, None] + t2[..., None, :]).reshape(n, PEER_HEADS, PEER_TOPK * PEER_TOPK)
    cidx = (i1[..., :, None] * PEER_NKEYS + i2[..., None, :]).reshape(n, PEER_HEADS, PEER_TOPK * PEER_TOPK)
    ts, j = lax.top_k(cand, PEER_TOPK)
    eidx = jnp.take_along_axis(cidx, j, axis=-1)
    g = jax.nn.softmax(ts, axis=-1)
    u = peer_u[eidx]
    v = peer_v[eidx]
    a = jax.nn.gelu(jnp.einsum('nd,nhed->nhe', x2, u).astype(jnp.float32), approximate=False)
    return jnp.einsum('nhe,nhed->nd', (g * a).astype(v.dtype), v)


def merge_and_channel_mix(x, conv_out, attn_out, g_conv, g_attn, w_out, ln1_g, ln1_b,
                          peer_w_query, peer_keys1, peer_keys2, peer_u, peer_v, ln2_g, ln2_b, blocked):
    m = jax.nn.sigmoid(g_conv) * conv_out + jax.nn.sigmoid(g_attn) * attn_out
    h = layer_norm(ALPHA * x + jnp.einsum('ntd,de->nte', m, w_out), ln1_g, ln1_b)
    N, T, D = h.shape
    h2 = h.reshape(N * T, D)
    fn = lambda hb: peer_ffn(hb, peer_w_query, peer_keys1, peer_keys2, peer_u, peer_v)
    if blocked:
        f = lax.map(fn, h2.reshape(-1, PEER_BLOCK, D)).reshape(N * T, D)
    else:
        f = fn(h2)
    return layer_norm(ALPHA * h + f.reshape(N, T, D), ln2_g, ln2_b)


def setup_inputs(seed: int = 0) -> dict:
    key = jax.random.key(seed)
    ks = jax.random.split(key, 24)
    n_pages = PAST_LEN // PAGE_SIZE
    n_used = DEC_BATCH * n_pages
    n_pool = n_used + max(1, n_used // 4)
    nrm = lambda k, shape: jax.random.normal(k, shape, dtype=jnp.float32)
    sizes, _ = _proj_layout()
    col_scale = jnp.concatenate([jnp.full((s,), sc, dtype=jnp.float32) for s, sc in
                                 zip(sizes, (1.0, 1.0, 1.0, BETA, 1.0, 1.0, 1.0, 1.0, 1.0))])
    page_table = jax.random.permutation(ks[0], n_pool)[:n_used].reshape(DEC_BATCH, n_pages).astype(jnp.int32)
    return {
        'x_prompt': nrm(ks[1], (BATCH, SEQ, D_MODEL)),
        'x_sample': nrm(ks[2], (DEC_BATCH, DEC_SEQ, D_MODEL)),
        'cache_k': nrm(ks[3], (n_pool, PAGE_SIZE, N_KV_HEADS, HEAD_DIM)),
        'cache_v': nrm(ks[4], (n_pool, PAGE_SIZE, N_KV_HEADS, HEAD_DIM)) * BETA,
        'cache_kidx': nrm(ks[5], (n_pool, PAGE_SIZE, IDX_DIM)),
        'state_conv': nrm(ks[6], (DEC_BATCH, CONV_W - 1, D_CONV)) * 0.5,
        'page_table': page_table,
        'w_in': nrm(ks[7], (D_MODEL, sum(sizes))) * D_MODEL ** -0.5 * col_scale,
        'conv_w': nrm(ks[8], (CONV_W, D_CONV)) * CONV_W ** -0.5,
        'conv_b': nrm(ks[9], (D_CONV,)) * 0.01,
        'conv_ln_g': 1.0 + 0.05 * nrm(ks[10], (D_CONV,)),
        'conv_ln_b': 0.01 * nrm(ks[11], (D_CONV,)),
        'w_conv_out': nrm(ks[12], (D_CONV, D_MODEL)) * D_CONV ** -0.5 * BETA,
        'w_out': nrm(ks[13], (D_MODEL, D_MODEL)) * D_MODEL ** -0.5 * BETA,
        'ln1_g': 1.0 + 0.05 * nrm(ks[14], (D_MODEL,)),
        'ln1_b': 0.01 * nrm(ks[15], (D_MODEL,)),
        'peer_w_query': nrm(ks[16], (D_MODEL, PEER_HEADS, PEER_DQ)) * D_MODEL ** -0.5,
        'peer_keys1': nrm(ks[17], (PEER_HEADS, PEER_NKEYS, PEER_DQ // 2)) * (PEER_DQ // 2) ** -0.5,
        'peer_keys2': nrm(ks[18], (PEER_HEADS, PEER_NKEYS, PEER_DQ // 2)) * (PEER_DQ // 2) ** -0.5,
        'peer_u': nrm(ks[19], (PEER_N_EXPERTS, D_MODEL)) * D_MODEL ** -0.5,
        'peer_v': nrm(ks[20], (PEER_N_EXPERTS, D_MODEL)) * BETA,
        'ln2_g': 1.0 + 0.05 * nrm(ks[21], (D_MODEL,)),
        'ln2_b': 0.01 * nrm(ks[22], (D_MODEL,)),
    }


def reference(x_prompt, x_sample, cache_k, cache_v, cache_kidx, state_conv, page_table,
              w_in, conv_w, conv_b, conv_ln_g, conv_ln_b, w_conv_out, w_out, ln1_g, ln1_b,
              peer_w_query, peer_keys1, peer_keys2, peer_u, peer_v, ln2_g, ln2_b):
    yp, ys = x_prompt, x_sample
    for _ in range(DEPTH):
        T = yp.shape[1]
        pos_p = jnp.arange(T, dtype=jnp.int32)
        u, q, k_p, v_p, qi, kidx_p, wi, gc, ga = project_inputs(yp, pos_p, w_in)
        prev0 = jnp.zeros((yp.shape[0], CONV_W - 1, D_CONV), u.dtype)
        conv_o, conv_p = conformer_conv(u, prev0, conv_w, conv_b, conv_ln_g, conv_ln_b, w_conv_out)
        attn_o = attn_prompt(q, k_p, v_p, qi, kidx_p, wi)
        yp = merge_and_channel_mix(yp, conv_o, attn_o, gc, ga, w_out, ln1_g, ln1_b, peer_w_query,
                                   peer_keys1, peer_keys2, peer_u, peer_v, ln2_g, ln2_b, True)
        past = page_table.shape[1] * PAGE_SIZE
        pos_s = past + jnp.arange(ys.shape[1], dtype=jnp.int32)
        u, q, k_s, v_s, qi, kidx_s, wi, gc, ga = project_inputs(ys, pos_s, w_in)
        conv_o, conv_s = conformer_conv(u, state_conv.astype(u.dtype), conv_w, conv_b, conv_ln_g,
                                        conv_ln_b, w_conv_out)
        attn_o = attn_sample(q, k_s, v_s, qi, kidx_s, wi, cache_k, cache_v, cache_kidx, page_table)
        ys = merge_and_channel_mix(ys, conv_o, attn_o, gc, ga, w_out, ln1_g, ln1_b, peer_w_query,
                                   peer_keys1, peer_keys2, peer_u, peer_v, ln2_g, ln2_b, False)
    return (yp, ys, k_p, v_p, kidx_p, conv_p, k_s, v_s, kidx_s, conv_s)
```

```python
import functools
import math

import jax
import jax.numpy as jnp
from jax import lax
from jax.experimental import pallas as pl
from jax.experimental.pallas import tpu as pltpu

f32 = jnp.float32
bf16 = jnp.bfloat16

D_CONV_W = 31
N_HEADS = 16
N_KV_HEADS = 4
HEAD_DIM = 64
IDX_HEADS = 8
IDX_DIM = 64
TOPK_MAX = 256
ROPE_THETA = 500000.0
ROPE_HALF = 8
PAGE = 128
PEER_HEADS = 8
PEER_NKEYS = 128
PEER_DQ = 128
PEER_TOPK = 16
ALPHA = 2.0 ** 0.25
LN_EPS = 1e-5

LANES = 128
VMEM_LIMIT = 56 * 1024 * 1024
NEG = -1e30
KEY_CHUNK = 512
PAGES_PER_STEP = 8


def _cparams(sem):
    return pltpu.CompilerParams(dimension_semantics=sem, vmem_limit_bytes=VMEM_LIMIT)


def _rope_tables(pos):
    inv = jnp.power(ROPE_THETA, -jnp.arange(ROPE_HALF, dtype=f32) / ROPE_HALF)
    ang = pos.astype(f32)[:, None] * inv
    cos, sin = jnp.cos(ang), jnp.sin(ang)
    d = jnp.arange(HEAD_DIM)
    cosp = jnp.where(d < 2 * ROPE_HALF, cos[:, d % ROPE_HALF], 1.0)
    sap = jnp.where(d < ROPE_HALF, -sin[:, d % ROPE_HALF], 0.0)
    sbp = jnp.where((d >= ROPE_HALF) & (d < 2 * ROPE_HALF), sin[:, d % ROPE_HALF], 0.0)
    one = jnp.ones_like(cosp)
    zero = jnp.zeros_like(cosp)
    return (jnp.concatenate([cosp, cosp, cosp, one], axis=1),
            jnp.concatenate([sap, sap, sap, zero], axis=1),
            jnp.concatenate([sbp, sbp, sbp, zero], axis=1))


def _rope(z, cos, sa, sb):
    outs = []
    for c in range(z.shape[1] // LANES):
        zc = z[:, c * LANES:(c + 1) * LANES]
        outs.append(zc * cos + pltpu.roll(zc, LANES - ROPE_HALF, 1) * sa + pltpu.roll(zc, ROPE_HALF, 1) * sb)
    return outs[0] if len(outs) == 1 else jnp.concatenate(outs, axis=1)


def _proj_kernel(x_ref, w_ref, cos_ref, sa_ref, sb_ref,
                 u_ref, q_ref, k_ref, v_ref, qi_ref, ki_ref, wi_ref, gc_ref, ga_ref):
    x = x_ref[...].astype(bf16)
    cos, sa, sb = cos_ref[:, :LANES], sa_ref[:, :LANES], sb_ref[:, :LANES]
    cos2, sa2, sb2 = cos_ref[:, LANES:], sa_ref[:, LANES:], sb_ref[:, LANES:]

    def mm(c0, c1):
        return jnp.dot(x, w_ref[:, c0:c1], preferred_element_type=f32)

    z = mm(0, 1024)
    u_ref[...] = z[:, :512] * jax.nn.sigmoid(z[:, 512:])
    z = mm(1024, 2048)
    q_ref[...] = (_rope(z, cos, sa, sb) * (HEAD_DIM ** -0.5)).astype(bf16)
    z = mm(2048, 2560)
    k_ref[...] = _rope(z[:, :256], cos, sa, sb)
    v_ref[...] = z[:, 256:]
    z = mm(2560, 3072)
    qi_ref[...] = _rope(z, cos, sa, sb).astype(bf16)
    z = mm(3072, 3200)
    z = _rope(z, cos2, sa2, sb2)
    ki_ref[...] = z[:, :IDX_DIM]
    wi_ref[...] = z[:, IDX_DIM:IDX_DIM + IDX_HEADS]
    gc_ref[...] = jax.nn.sigmoid(mm(3200, 4224))
    ga_ref[...] = jax.nn.sigmoid(mm(4224, 5248))


def _project(x2, w_pad, tabs, tm):
    R, D = x2.shape
    rt = tabs[0].shape[0]
    nt = rt // tm
    row = lambda i: (i, 0)
    tab = lambda i: (i % nt, 0)
    widths = [(512, f32), (1024, bf16), (256, f32), (256, f32), (512, bf16), (IDX_DIM, f32), (IDX_HEADS, f32),
              (1024, f32), (1024, f32)]
    return pl.pallas_call(
        _proj_kernel,
        grid=(R // tm,),
        in_specs=[pl.BlockSpec((tm, D), row),
                  pl.BlockSpec(w_pad.shape, lambda i: (0, 0)),
                  pl.BlockSpec((tm, 256), tab), pl.BlockSpec((tm, 256), tab), pl.BlockSpec((tm, 256), tab)],
        out_specs=[pl.BlockSpec((tm, w), row) for w, _ in widths],
        out_shape=[jax.ShapeDtypeStruct((R, w), dt) for w, dt in widths],
        compiler_params=_cparams(("parallel",)),
        name="proj",
    )(x2, w_pad, *tabs)


def _select_tau(S, n_ch, k, n_adm):
    _, RW, CH = S.shape
    kf = float(k)
    inf = float("inf")

    def mm_body(c, carry):
        mn, mx = carry
        s = S[c]
        mx = jnp.maximum(mx, jnp.max(s, axis=1, keepdims=True))
        mn = jnp.minimum(mn, jnp.min(jnp.where(s == -inf, inf, s), axis=1, keepdims=True))
        return mn, mx

    mn, mx = lax.fori_loop(0, n_ch, mm_body, (jnp.full((RW, 1), inf, f32), jnp.full((RW, 1), -inf, f32)))
    hi0 = mx + jnp.maximum(jnp.abs(mx), 1e-30) * (2.0 ** -20)

    def count(t, strict):
        tb = jnp.broadcast_to(t, (RW, LANES))

        def body(c, acc):
            s = S[c]
            for l in range(CH // LANES):
                sl = s[:, l * LANES:(l + 1) * LANES]
                hit = (sl > tb) if strict else (sl >= tb)
                acc = acc + jnp.where(hit, 1.0, 0.0)
            return acc

        acc = lax.fori_loop(0, n_ch, body, jnp.zeros((RW, LANES), f32))
        return jnp.sum(acc, axis=1, keepdims=True)

    all_sel = n_adm <= kf
    done0 = jnp.where(all_sel, 1.0, 0.0)
    tau0 = jnp.full((RW, 1), -3e38, f32)
    tie0 = jnp.zeros((RW, 1), f32)

    def cond(st):
        return jnp.logical_and(jnp.min(st[4]) < 0.5, st[6] < 400)

    def body(st):
        lo, hi, cnt_lo, tau, done, tie, it = st
        mid = lo + (hi - lo) * 0.5
        cnt = count(mid, False)
        conv = (mid <= lo) | (mid >= hi)
        exact = cnt == kf
        ge = cnt >= kf
        was = done > 0.5
        tau_n = jnp.where(was, tau, jnp.where(exact, mid, lo))
        tie_n = jnp.where(was, tie, jnp.where(conv & (~exact) & (cnt_lo > kf), 1.0, 0.0))
        done_n = jnp.where(was | exact | conv, 1.0, 0.0)
        return (jnp.where(ge, mid, lo), jnp.where(ge, hi, mid), jnp.where(ge, cnt, cnt_lo),
                tau_n, done_n, tie_n, it + 1)

    st = lax.while_loop(cond, body, (mn, hi0, n_adm, tau0, done0, tie0, jnp.int32(0)))
    tau, tie = st[3], st[5]

    @pl.when(jnp.max(tie) > 0.5)
    def _():
        need = kf - count(tau, True)
        r = lax.broadcasted_iota(jnp.int32, (CH, CH), 0)
        c_ = lax.broadcasted_iota(jnp.int32, (CH, CH), 1)
        tri = jnp.where(r <= c_, 1.0, 0.0).astype(bf16)

        def body(c, run):
            s = S[c]
            eq = (s == tau) & (tie > 0.5)
            eqf = jnp.where(eq, 1.0, 0.0)
            pref = jnp.dot(eqf.astype(bf16), tri, preferred_element_type=f32) + run
            S[c] = jnp.where(eq & (pref > need), -inf, s)
            return run + jnp.sum(eqf, axis=1, keepdims=True)

        lax.fori_loop(0, n_ch, body, jnp.zeros((RW, 1), f32))

    return tau


def _attn_prompt_kernel(qi_ref, wi_ref, kit_ref, q_ref, kt_ref, v_ref, o_ref, S, *, topk):
    qb = pl.program_id(1)
    QB = q_ref.shape[2]
    CH = S.shape[2]
    n_ch = (qb * QB + QB + CH - 1) // CH
    qpos = qb * QB + lax.broadcasted_iota(jnp.int32, (QB, 1), 0)
    wsc = wi_ref[0] * ((IDX_HEADS * IDX_DIM) ** -0.5)
    qi = qi_ref[0].reshape(IDX_HEADS * QB, IDX_DIM)

    def score_chunk(c, carry):
        s = jnp.dot(qi, kit_ref[0, c], preferred_element_type=f32)
        s = jnp.maximum(s, 0.0).reshape(IDX_HEADS, QB, CH)
        sc = s[0] * wsc[:, 0:1]
        for h in range(1, IDX_HEADS):
            sc = sc + s[h] * wsc[:, h:h + 1]
        kpos = c * CH + lax.broadcasted_iota(jnp.int32, (1, CH), 1)
        S[c] = jnp.where(kpos <= qpos, sc, -float("inf"))
        return carry

    lax.fori_loop(0, n_ch, score_chunk, 0)
    tau = _select_tau(S, n_ch, topk, (qpos + 1).astype(f32))

    G = N_HEADS // N_KV_HEADS
    for g in range(N_KV_HEADS):
        qg = q_ref[0, g * G:(g + 1) * G].reshape(G * QB, HEAD_DIM)

        def chunk(c, carry):
            m, l, acc = carry
            s = jnp.dot(qg, kt_ref[0, c, g], preferred_element_type=f32)
            sel = S[c] >= tau
            s = jnp.where(sel[None], s.reshape(G, QB, CH), NEG).reshape(G * QB, CH)
            m_new = jnp.maximum(m, jnp.max(s, axis=1, keepdims=True))
            p = jnp.exp(s - m_new)
            a = jnp.exp(m - m_new)
            l = a * l + jnp.sum(p, axis=1, keepdims=True)
            off = pl.multiple_of(c * CH, CH)
            acc = a * acc + jnp.dot(p.astype(bf16), v_ref[0, g, pl.ds(off, CH), :], preferred_element_type=f32)
            return m_new, l, acc

        m, l, acc = lax.fori_loop(0, n_ch, chunk, (jnp.full((G * QB, 1), NEG, f32),
                                                   jnp.zeros((G * QB, 1), f32),
                                                   jnp.zeros((G * QB, HEAD_DIM), f32)))
        o_ref[0, g * G:(g + 1) * G] = (acc / l).reshape(G, QB, HEAD_DIM)


def _attn_prompt(q, k, v, qi, ki, wi, QB=128):
    N, T, _ = q.shape
    topk = min(TOPK_MAX, T // 4)
    CH = min(KEY_CHUNK, T)
    NC = T // CH
    qh = q.reshape(N, T, N_HEADS, HEAD_DIM).transpose(0, 2, 1, 3)
    qih = qi.reshape(N, T, IDX_HEADS, IDX_DIM).transpose(0, 2, 1, 3)
    kit = ki.astype(bf16).reshape(N, NC, CH, IDX_DIM).transpose(0, 1, 3, 2)
    kt = k.astype(bf16).reshape(N, NC, CH, N_KV_HEADS, HEAD_DIM).transpose(0, 1, 3, 4, 2)
    vh = v.astype(bf16).reshape(N, T, N_KV_HEADS, HEAD_DIM).transpose(0, 2, 1, 3)
    o = pl.pallas_call(
        functools.partial(_attn_prompt_kernel, topk=topk),
        grid=(N, T // QB),
        in_specs=[pl.BlockSpec((1, IDX_HEADS, QB, IDX_DIM), lambda n, b: (n, 0, b, 0)),
                  pl.BlockSpec((1, QB, IDX_HEADS), lambda n, b: (n, b, 0)),
                  pl.BlockSpec((1, NC, IDX_DIM, CH), lambda n, b: (n, 0, 0, 0)),
                  pl.BlockSpec((1, N_HEADS, QB, HEAD_DIM), lambda n, b: (n, 0, b, 0)),
                  pl.BlockSpec((1, NC, N_KV_HEADS, HEAD_DIM, CH), lambda n, b: (n, 0, 0, 0, 0)),
                  pl.BlockSpec((1, N_KV_HEADS, T, HEAD_DIM), lambda n, b: (n, 0, 0, 0))],
        out_specs=pl.BlockSpec((1, N_HEADS, QB, HEAD_DIM), lambda n, b: (n, 0, b, 0)),
        out_shape=jax.ShapeDtypeStruct((N, N_HEADS, T, HEAD_DIM), f32),
        scratch_shapes=[pltpu.VMEM((NC, QB, CH), f32)],
        compiler_params=_cparams(("parallel", "arbitrary")),
        name="attn_prompt",
    )(qih, wi, kit, qh, kt, vh)
    return o.transpose(0, 2, 1, 3).reshape(N, T, N_HEADS * HEAD_DIM)


def _sample_scores_kernel(pt_ref, qi_ref, w_ref, kin_ref, *rest):
    pages = rest[:PAGES_PER_STEP]
    sp_ref, sn_ref = rest[PAGES_PER_STEP:]
    TQ = sp_ref.shape[2]
    qi = qi_ref[0]
    w = w_ref[0] * ((IDX_HEADS * IDX_DIM) ** -0.5)
    nt = (((1,), (1,)), ((), ()))

    def scores(keys):
        s = lax.dot_general(qi, keys, nt, preferred_element_type=f32)
        s = jnp.maximum(s, 0.0) * w
        return jnp.sum(s.reshape(TQ, IDX_HEADS, s.shape[1]), axis=1)

    keys = jnp.concatenate([p[0].astype(bf16) for p in pages], axis=0)
    sp_ref[0, 0] = scores(keys)

    @pl.when(pl.program_id(1) == 0)
    def _():
        s = scores(kin_ref[0])
        t = lax.broadcasted_iota(jnp.int32, s.shape, 0)
        j = lax.broadcasted_iota(jnp.int32, s.shape, 1)
        sn_ref[0] = jnp.where(j <= t, s, -float("inf"))


def _sample_select_kernel(sp_ref, sn_ref, bias_ref, S, *, topk, past, tq):
    NCP = sp_ref.shape[0]
    RW, CH = S.shape[1], S.shape[2]
    for c in range(NCP):
        S[c] = sp_ref[c]
    S[NCP] = jnp.concatenate([sn_ref[...], jnp.full((RW, CH - LANES), -float("inf"), f32)], axis=1)
    t = lax.broadcasted_iota(jnp.int32, (RW, 1), 0) % tq
    tau = _select_tau(S, NCP + 1, topk, (past + 1 + t).astype(f32))
    for c in range(NCP + 1):
        bias_ref[c] = jnp.where(S[c] >= tau, 0.0, NEG)


def _sample_attend_kernel(pt_ref, q_ref, bp_ref, bn_ref, kn_ref, vn_ref, *rest):
    kp = rest[:PAGES_PER_STEP]
    vp = rest[PAGES_PER_STEP:2 * PAGES_PER_STEP]
    o_ref, m_ref, l_ref, acc_ref = rest[2 * PAGES_PER_STEP:]
    p_id = pl.program_id(1)
    q = q_ref[0]
    R = q.shape[0]
    TQ = bp_ref.shape[2]
    nt = (((1,), (1,)), ((), ()))

    @pl.when(p_id == 0)
    def _():
        m_ref[...] = jnp.full(m_ref.shape, NEG, f32)
        l_ref[...] = jnp.zeros(l_ref.shape, f32)
        acc_ref[...] = jnp.zeros(acc_ref.shape, f32)

    def step(keys, vals, bias):
        s = lax.dot_general(q, keys, nt, preferred_element_type=f32)
        L = s.shape[1]
        b = jnp.broadcast_to(bias[:, None, :], (TQ, R // TQ, L)).reshape(R, L)
        s = jnp.where(b < 0.0, NEG, s)
        m = m_ref[...]
        m_new = jnp.maximum(m, jnp.max(s, axis=1, keepdims=True))
        p = jnp.exp(s - m_new)
        a = jnp.exp(m - m_new)
        l_ref[...] = a * l_ref[...] + jnp.sum(p, axis=1, keepdims=True)
        acc_ref[...] = a * acc_ref[...] + jnp.dot(p.astype(bf16), vals, preferred_element_type=f32)
        m_ref[...] = m_new

    keys = jnp.concatenate([p[0].astype(bf16) for p in kp], axis=0)
    vals = jnp.concatenate([p[0].astype(bf16) for p in vp], axis=0)
    step(keys, vals, bp_ref[0, 0])

    @pl.when(p_id == pl.num_programs(1) - 1)
    def _():
        step(kn_ref[0], vn_ref[0], bn_ref[0])
        o_ref[0] = acc_ref[...] / l_ref[...]


def _attn_sample(q, k_new, v_new, qi, ki_new, wi, cache_k, cache_v, cache_kidx, page_table):
    N, tq, _ = q.shape
    n_pages = page_table.shape[1]
    past = n_pages * PAGE
    topk = min(TOPK_MAX, (past + tq) // 4)
    PP = PAGES_PER_STEP
    NP = n_pages // PP
    LP = PP * PAGE
    n_pool = cache_k.shape[0]
    KV = N_KV_HEADS * HEAD_DIM

    qi2 = qi.reshape(N, tq * IDX_HEADS, IDX_DIM)
    w2 = wi.reshape(N, tq * IDX_HEADS, 1)
    pad_rows = lambda a: jnp.pad(a, ((0, 0), (0, LANES - tq), (0, 0)))
    kin = pad_rows(ki_new.astype(bf16))

    def page_spec(shape, j):
        return pl.BlockSpec(shape, lambda n, p, pt: (pt[n, p * PP + j], 0, 0))

    sp, sn = pl.pallas_call(
        _sample_scores_kernel,
        grid_spec=pltpu.PrefetchScalarGridSpec(
            num_scalar_prefetch=1, grid=(N, NP),
            in_specs=[pl.BlockSpec((1, tq * IDX_HEADS, IDX_DIM), lambda n, p, pt: (n, 0, 0)),
                      pl.BlockSpec((1, tq * IDX_HEADS, 1), lambda n, p, pt: (n, 0, 0)),
                      pl.BlockSpec((1, LANES, IDX_DIM), lambda n, p, pt: (n, 0, 0))]
                     + [page_spec((1, PAGE, IDX_DIM), j) for j in range(PP)],
            out_specs=[pl.BlockSpec((1, 1, tq, LP), lambda n, p, pt: (n, p, 0, 0)),
                       pl.BlockSpec((1, tq, LANES), lambda n, p, pt: (n, 0, 0))]),
        out_shape=[jax.ShapeDtypeStruct((N, NP, tq, LP), f32), jax.ShapeDtypeStruct((N, tq, LANES), f32)],
        compiler_params=_cparams(("parallel", "arbitrary")),
        name="sample_scores",
    )(page_table, qi2, w2, kin, *([cache_kidx] * PP))

    CH = KEY_CHUNK
    NCP = past // CH
    RW = N * tq
    sp2 = sp.transpose(0, 2, 1, 3).reshape(RW, NCP, CH).transpose(1, 0, 2)
    bias = pl.pallas_call(
        functools.partial(_sample_select_kernel, topk=topk, past=past, tq=tq),
        out_shape=jax.ShapeDtypeStruct((NCP + 1, RW, CH), f32),
        scratch_shapes=[pltpu.VMEM((NCP + 1, RW, CH), f32)],
        compiler_params=pltpu.CompilerParams(vmem_limit_bytes=VMEM_LIMIT),
        name="sample_select",
    )(sp2, sn.reshape(RW, LANES))
    bp = bias[:NCP].transpose(1, 0, 2).reshape(N, tq, NP, LP).transpose(0, 2, 1, 3)
    bn = bias[NCP, :, :LANES].reshape(N, tq, LANES)

    G = N_HEADS // N_KV_HEADS
    q5 = q.reshape(N, tq, N_KV_HEADS, G, 1, HEAD_DIM)
    eye = jnp.eye(N_KV_HEADS, dtype=q.dtype).reshape(1, 1, N_KV_HEADS, 1, N_KV_HEADS, 1)
    qbd = (q5 * eye).reshape(N, tq * N_HEADS, KV)
    kn = pad_rows(k_new.astype(bf16))
    vn = pad_rows(v_new.astype(bf16))
    ck = cache_k.reshape(n_pool, PAGE, KV)
    cv = cache_v.reshape(n_pool, PAGE, KV)
    R = tq * N_HEADS
    o = pl.pallas_call(
        _sample_attend_kernel,
        grid_spec=pltpu.PrefetchScalarGridSpec(
            num_scalar_prefetch=1, grid=(N, NP),
            in_specs=[pl.BlockSpec((1, R, KV), lambda n, p, pt: (n, 0, 0)),
                      pl.BlockSpec((1, 1, tq, LP), lambda n, p, pt: (n, p, 0, 0)),
                      pl.BlockSpec((1, tq, LANES), lambda n, p, pt: (n, 0, 0)),
                      pl.BlockSpec((1, LANES, KV), lambda n, p, pt: (n, 0, 0)),
                      pl.BlockSpec((1, LANES, KV), lambda n, p, pt: (n, 0, 0))]
                     + [page_spec((1, PAGE, KV), j) for j in range(PP)]
                     + [page_spec((1, PAGE, KV), j) for j in range(PP)],
            out_specs=pl.BlockSpec((1, R, KV), lambda n, p, pt: (n, 0, 0)),
            scratch_shapes=[pltpu.VMEM((R, 1), f32), pltpu.VMEM((R, 1), f32), pltpu.VMEM((R, KV), f32)]),
        out_shape=jax.ShapeDtypeStruct((N, R, KV), f32),
        compiler_params=_cparams(("parallel", "arbitrary")),
        name="sample_attend",
    )(page_table, qbd, bp, bn, kn, vn, *([ck] * PP), *([cv] * PP))
    o6 = o.reshape(N, tq, N_KV_HEADS, G, N_KV_HEADS, HEAD_DIM)
    sel = jnp.eye(N_KV_HEADS, dtype=f32).reshape(1, 1, N_KV_HEADS, 1, N_KV_HEADS, 1)
    return jnp.sum(o6 * sel, axis=4).reshape(N, tq, N_HEADS * HEAD_DIM)


HALO = 32


def _conv_kernel(u_ref, prev_ref, cw_ref, cb_ref, g_ref, b_ref, wo_ref, o_ref, buf):
    tt = u_ref.shape[1]

    @pl.when(pl.program_id(1) == 0)
    def _():
        buf[0:HALO] = prev_ref[0]

    @pl.when(pl.program_id(1) > 0)
    def _():
        buf[0:HALO] = buf[tt:tt + HALO]

    buf[HALO:HALO + tt] = u_ref[0]
    off = HALO - (D_CONV_W - 1)
    y = buf[off:off + tt] * cw_ref[0:1, :]
    for j in range(1, D_CONV_W):
        y = y + buf[off + j:off + j + tt] * cw_ref[j:j + 1, :]
    y = y + cb_ref[...]
    mu = jnp.mean(y, axis=-1, keepdims=True)
    yc = y - mu
    var = jnp.mean(yc * yc, axis=-1, keepdims=True)
    y = yc * lax.rsqrt(var + LN_EPS) * g_ref[...] + b_ref[...]
    y = y * jax.nn.sigmoid(y)
    o_ref[0] = jnp.dot(y.astype(bf16), wo_ref[...], preferred_element_type=f32)


def _conv_module(u, prev, conv_w, conv_b, ln_g, ln_b, w_conv_out_bf, tt):
    N, T, C = u.shape
    D = w_conv_out_bf.shape[1]
    prev_pad = jnp.pad(prev, ((0, 0), (HALO - prev.shape[1], 0), (0, 0)))
    cw = jnp.pad(conv_w, ((0, 32 - conv_w.shape[0]), (0, 0)))
    row2 = lambda a: a.reshape(1, -1)
    return pl.pallas_call(
        _conv_kernel,
        grid=(N, T // tt),
        in_specs=[pl.BlockSpec((1, tt, C), lambda n, t: (n, t, 0)),
                  pl.BlockSpec((1, HALO, C), lambda n, t: (n, 0, 0)),
                  pl.BlockSpec((32, C), lambda n, t: (0, 0)),
                  pl.BlockSpec((1, C), lambda n, t: (0, 0)),
                  pl.BlockSpec((1, C), lambda n, t: (0, 0)),
                  pl.BlockSpec((1, C), lambda n, t: (0, 0)),
                  pl.BlockSpec((C, D), lambda n, t: (0, 0))],
        out_specs=pl.BlockSpec((1, tt, D), lambda n, t: (n, t, 0)),
        out_shape=jax.ShapeDtypeStruct((N, T, D), f32),
        scratch_shapes=[pltpu.VMEM((HALO + tt + 8, C), f32)],
        compiler_params=_cparams(("parallel", "arbitrary")),
        name="conv_module",
    )(u, prev_pad, cw, row2(conv_b), row2(ln_g), row2(ln_b), w_conv_out_bf)


def _merge_kernel(x_ref, co_ref, ao_ref, gc_ref, ga_ref, wo_ref, g_ref, b_ref, h_ref):
    m = gc_ref[...] * co_ref[...] + ga_ref[...] * ao_ref[...]
    z = ALPHA * x_ref[...] + jnp.dot(m.astype(bf16), wo_ref[...], preferred_element_type=f32)
    mu = jnp.mean(z, axis=-1, keepdims=True)
    zc = z - mu
    var = jnp.mean(zc * zc, axis=-1, keepdims=True)
    h_ref[...] = zc * lax.rsqrt(var + LN_EPS) * g_ref[...] + b_ref[...]


def _merge(x2, conv_o, attn_o, sgc, sga, w_out_bf, ln_g, ln_b, tm):
    R, D = x2.shape
    row = lambda i: (i, 0)
    cst = lambda i: (0, 0)
    return pl.pallas_call(
        _merge_kernel,
        grid=(R // tm,),
        in_specs=[pl.BlockSpec((tm, D), row)] * 5
                 + [pl.BlockSpec((D, D), cst), pl.BlockSpec((1, D), cst), pl.BlockSpec((1, D), cst)],
        out_specs=pl.BlockSpec((tm, D), row),
        out_shape=jax.ShapeDtypeStruct((R, D), f32),
        compiler_params=_cparams(("parallel",)),
        name="merge",
    )(x2, conv_o, attn_o, sgc, sga, w_out_bf, ln_g.reshape(1, D), ln_b.reshape(1, D))


def _top16(s):
    K, Tn = s.shape
    rows = lax.broadcasted_iota(jnp.int32, (K, Tn), 0).astype(f32)
    rank = jnp.full((K, Tn), float(PEER_TOPK), f32)
    tops = []
    for r in range(PEER_TOPK):
        m = jnp.max(s, axis=0, keepdims=True)
        first = jnp.min(jnp.where(s == m, rows, float(K)), axis=0, keepdims=True)
        pick = rows == first
        rank = jnp.where(pick, float(r), rank)
        s = jnp.where(pick, -float("inf"), s)
        tops.append(m)
    return rank, jnp.concatenate(tops, axis=0)


def _route_kernel(ht_ref, wq_ref, k1_ref, k2_ref, r2_ref, e2_ref, c_ref, e1_ref):
    Tn = ht_ref.shape[1]
    qh = jnp.dot(wq_ref[...], ht_ref[...], preferred_element_type=f32)
    half = PEER_DQ // 2
    T16 = PEER_TOPK
    ninf = -float("inf")
    sub = lax.broadcasted_iota(jnp.int32, (8, Tn), 0)
    subf = sub.astype(f32)
    for h in range(PEER_HEADS):
        q1 = qh[h * PEER_DQ:h * PEER_DQ + half].astype(bf16)
        q2 = qh[h * PEER_DQ + half:(h + 1) * PEER_DQ].astype(bf16)
        s1 = jnp.dot(k1_ref[h], q1, preferred_element_type=f32)
        s2 = jnp.dot(k2_ref[h], q2, preferred_element_type=f32)
        r1, t1 = _top16(s1)
        r2, t2 = _top16(s2)
        limits = [16, 8, 5, 4, 3, 2, 2, 2]
        cands, idxs = [], []
        cands.append(t1[0:1] + t2[0:8]); idxs.append(subf)
        cands.append(t1[0:1] + t2[8:16]); idxs.append(subf + 8.0)
        for a in range(1, 8):
            cands.append(jnp.where(sub < limits[a], t1[a:a + 1] + t2[0:8], ninf))
            idxs.append(subf + float(16 * a))
        cands.append(t1[8:16] + t2[0:1]); idxs.append((subf + 8.0) * 16.0)
        cand = jnp.concatenate(cands, axis=0)
        cidx = jnp.concatenate(idxs, axis=0)
        cmax = cand[0:1]
        ecand = jnp.exp(cand - cmax)
        picked = jnp.zeros(cand.shape, f32)
        for _ in range(T16):
            m = jnp.max(cand, axis=0, keepdims=True)
            first = jnp.min(jnp.where(cand == m, cidx, 4096.0), axis=0, keepdims=True)
            pick = cidx == first
            picked = jnp.where(pick, 1.0, picked)
            cand = jnp.where(pick, ninf, cand)
        z = jnp.sum(picked * ecand, axis=0, keepdims=True)
        la = [jnp.sum(picked[0:16], axis=0, keepdims=True)]
        for a in range(1, 8):
            la.append(jnp.sum(picked[8 + 8 * a:16 + 8 * a], axis=0, keepdims=True))
        ltail = picked[72:80]
        c = jnp.zeros(s1.shape, f32)
        for a in range(8):
            c = jnp.where(r1 == float(a), la[a], c)
        for a in range(8, 16):
            c = jnp.where(r1 == float(a), ltail[a - 8:a - 7], c)
        e1 = jnp.where(r1 < float(T16), jnp.exp(s1 - t1[0:1]), 0.0) / z
        e2 = jnp.where(r2 < float(T16), jnp.exp(s2 - t2[0:1]), 0.0)
        r2_ref[h] = r2
        e2_ref[h] = e2
        c_ref[h] = c
        e1_ref[h] = e1


def _route(ht_bf, wqt_bf, k1_bf, k2_bf, tn):
    D, R = ht_bf.shape
    shp = jax.ShapeDtypeStruct((PEER_HEADS, PEER_NKEYS, R), f32)
    ospec = pl.BlockSpec((PEER_HEADS, PEER_NKEYS, tn), lambda i: (0, 0, i))
    return pl.pallas_call(
        _route_kernel,
        grid=(R // tn,),
        in_specs=[pl.BlockSpec((D, tn), lambda i: (0, i)),
                  pl.BlockSpec(wqt_bf.shape, lambda i: (0, 0)),
                  pl.BlockSpec(k1_bf.shape, lambda i: (0, 0, 0)),
                  pl.BlockSpec(k2_bf.shape, lambda i: (0, 0, 0))],
        out_specs=[ospec] * 4,
        out_shape=[shp] * 4,
        compiler_params=_cparams(("parallel",)),
        name="peer_route",
    )(ht_bf, wqt_bf, k1_bf, k2_bf)


ET = 1024


def _peer_kernel(htb_ref, ht_ref, u_ref, vt_ref, r2_ref, e2_ref, c_ref, e1_ref, g_ref, b_ref,
                 y_ref, acc_ref, p_ref):
    et = pl.program_id(1)
    NK = PEER_NKEYS

    @pl.when(et == 0)
    def _():
        acc_ref[...] = jnp.zeros(acc_ref.shape, f32)

    hb = htb_ref[...]
    for ii in range(ET // NK):
        i = et * (ET // NK) + ii
        a = jnp.dot(u_ref[ii * NK:(ii + 1) * NK, :], hb, preferred_element_type=f32)
        gl = 0.5 * a * (1.0 + lax.erf(a * (2.0 ** -0.5)))
        w = jnp.zeros(a.shape, f32)
        for h in range(PEER_HEADS):
            ci = c_ref[h, pl.ds(i, 1), :]
            e1 = e1_ref[h, pl.ds(i, 1), :]
            w = w + jnp.where(r2_ref[h] < ci, e1 * e2_ref[h], 0.0)
        p_ref[ii * NK:(ii + 1) * NK, :] = (w * gl).astype(bf16)
    acc_ref[...] += jnp.dot(vt_ref[...], p_ref[...], preferred_element_type=f32)

    @pl.when(et == pl.num_programs(1) - 1)
    def _():
        z = ALPHA * ht_ref[...] + acc_ref[...]
        mu = jnp.mean(z, axis=0, keepdims=True)
        zc = z - mu
        var = jnp.mean(zc * zc, axis=0, keepdims=True)
        y_ref[...] = zc * lax.rsqrt(var + LN_EPS) * g_ref[...] + b_ref[...]


def _peer_dense(ht_bf, ht, u_bf, vt_bf, r2, e2, c, e1, ln_g, ln_b, tn):
    D, R = ht.shape
    E = u_bf.shape[0]
    tok = lambda i, e: (0, i)
    rt = pl.BlockSpec((PEER_HEADS, PEER_NKEYS, tn), lambda i, e: (0, 0, i))
    return pl.pallas_call(
        _peer_kernel,
        grid=(R // tn, E // ET),
        in_specs=[pl.BlockSpec((D, tn), tok), pl.BlockSpec((D, tn), tok),
                  pl.BlockSpec((ET, D), lambda i, e: (e, 0)),
                  pl.BlockSpec((D, ET), lambda i, e: (0, e)),
                  rt, rt, rt, rt,
                  pl.BlockSpec((D, 1), lambda i, e: (0, 0)), pl.BlockSpec((D, 1), lambda i, e: (0, 0))],
        out_specs=pl.BlockSpec((D, tn), tok),
        out_shape=jax.ShapeDtypeStruct((D, R), f32),
        scratch_shapes=[pltpu.VMEM((D, tn), f32), pltpu.VMEM((ET, tn), bf16)],
        compiler_params=_cparams(("parallel", "arbitrary")),
        name="peer_dense",
    )(ht_bf, ht, u_bf, vt_bf, r2, e2, c, e1, ln_g.reshape(D, 1), ln_b.reshape(D, 1))


def _channel_mix(h2, wqt_bf, k1_bf, k2_bf, u_bf, vt_bf, ln_g, ln_b, tn_route, tn_dense):
    ht = h2.T
    ht_bf = ht.astype(bf16)
    r2, e2, c, e1 = _route(ht_bf, wqt_bf, k1_bf, k2_bf, tn_route)
    yt = _peer_dense(ht_bf, ht, u_bf, vt_bf, r2, e2, c, e1, ln_g, ln_b, tn_dense)
    return yt.T


def _prep_weights(w_in, w_conv_out, w_out, peer_w_query, peer_keys1, peer_keys2, peer_u, peer_v):
    D = w_in.shape[0]
    split = 2 * (D // 2) + N_HEADS * HEAD_DIM + 2 * N_KV_HEADS * HEAD_DIM + IDX_HEADS * IDX_DIM + IDX_DIM + IDX_HEADS
    pad = (-split) % LANES
    w_pad = jnp.concatenate([w_in[:, :split], jnp.zeros((D, pad), w_in.dtype), w_in[:, split:]], axis=1).astype(bf16)
    wqt = peer_w_query.reshape(D, PEER_HEADS * PEER_DQ).T.astype(bf16)
    return dict(w_pad=w_pad, wco=w_conv_out.astype(bf16), wo=w_out.astype(bf16), wqt=wqt,
                k1=peer_keys1.astype(bf16), k2=peer_keys2.astype(bf16),
                u=peer_u.astype(bf16), vt=peer_v.T.astype(bf16))


def _group(x, pos, prev, W, conv_w, conv_b, conv_ln_g, conv_ln_b, ln1_g, ln1_b, ln2_g, ln2_b,
           attn_fn, tm, tt, tn_route, tn_dense):
    N, T, D = x.shape
    R = N * T
    x2 = x.reshape(R, D)
    tabs = _rope_tables(pos if T % tm == 0 else jnp.tile(pos, tm // T))
    u, q, k, v, qi, ki, wi, sgc, sga = _project(x2, W["w_pad"], tabs, tm)
    r3 = lambda a: a.reshape(N, T, a.shape[-1])
    attn_o = attn_fn(r3(q), r3(k), r3(v), r3(qi), r3(ki), r3(wi))
    u3 = r3(u)
    if T % tt == 0:
        conv_o = _conv_module(u3, prev, conv_w, conv_b, conv_ln_g, conv_ln_b, W["wco"], tt)
    else:
        up = jnp.pad(u3, ((0, 0), (0, tt - T), (0, 0)))
        conv_o = _conv_module(up, prev, conv_w, conv_b, conv_ln_g, conv_ln_b, W["wco"], tt)[:, :T]
    h2 = _merge(x2, conv_o.reshape(R, D), attn_o.reshape(R, D), sgc, sga, W["wo"], ln1_g, ln1_b, tm)
    y2 = _channel_mix(h2, W["wqt"], W["k1"], W["k2"], W["u"], W["vt"], ln2_g, ln2_b, tn_route, tn_dense)
    return y2.reshape(N, T, D), k, v, ki, u3


def kernel(x_prompt, x_sample, cache_k, cache_v, cache_kidx, state_conv, page_table, w_in, conv_w, conv_b,
           conv_ln_g, conv_ln_b, w_conv_out, w_out, ln1_g, ln1_b, peer_w_query, peer_keys1, peer_keys2,
           peer_u, peer_v, ln2_g, ln2_b):
    W = _prep_weights(w_in, w_conv_out, w_out, peer_w_query, peer_keys1, peer_keys2, peer_u, peer_v)
    common = (W, conv_w, conv_b, conv_ln_g, conv_ln_b, ln1_g, ln1_b, ln2_g, ln2_b)
    C = conv_w.shape[1]
    keep = D_CONV_W - 1

    N, T, D = x_prompt.shape
    tm = min(256, N * T)
    yp, k_p, v_p, kidx_p, u_p = _group(
        x_prompt, jnp.arange(T, dtype=jnp.int32), jnp.zeros((N, keep, C), f32), *common,
        _attn_prompt, tm, min(512, T), min(256, N * T), min(512, N * T))
    conv_p = u_p[:, T - keep:]

    NS, tq, _ = x_sample.shape
    past = page_table.shape[1] * PAGE
    attn_s = functools.partial(_attn_sample, cache_k=cache_k, cache_v=cache_v, cache_kidx=cache_kidx,
                               page_table=page_table)
    rs = NS * tq
    ys, k_s, v_s, kidx_s, u_s = _group(
        x_sample, past + jnp.arange(tq, dtype=jnp.int32), state_conv, *common,
        lambda q, k, v, qi, ki, wi: attn_s(q, k, v, qi, ki, wi), rs, 8, rs, rs)
    conv_s = jnp.concatenate([state_conv, u_s], axis=1)[:, -keep:]

    kv4 = lambda a, n, t: a.reshape(n, t, N_KV_HEADS, HEAD_DIM)
    return (yp, ys, kv4(k_p, N, T), kv4(v_p, N, T), kidx_p.reshape(N, T, IDX_DIM), conv_p,
            kv4(k_s, NS, tq), kv4(v_s, NS, tq), kidx_s.reshape(NS, tq, IDX_DIM), conv_s)
```

```python
import functools
import math

import jax
import jax.numpy as jnp
from jax import lax
from jax.experimental import pallas as pl
from jax.experimental.pallas import tpu as pltpu

f32 = jnp.float32
bf16 = jnp.bfloat16

D_CONV_W = 31
N_HEADS = 16
N_KV_HEADS = 4
HEAD_DIM = 64
IDX_HEADS = 8
IDX_DIM = 64
TOPK_MAX = 256
ROPE_THETA = 500000.0
ROPE_HALF = 8
PAGE = 128
PEER_HEADS = 8
PEER_NKEYS = 128
PEER_DQ = 128
PEER_TOPK = 16
ALPHA = 2.0 ** 0.25
LN_EPS = 1e-5

LANES = 128
VMEM_LIMIT = 56 * 1024 * 1024
NEG = -1e30
KEY_CHUNK = 512
Q_ROWS = 256
PAGES_PER_STEP = 8


def _cparams(sem):
    return pltpu.CompilerParams(dimension_semantics=sem, vmem_limit_bytes=VMEM_LIMIT)


def _rope_tables(pos):
    inv = jnp.power(ROPE_THETA, -jnp.arange(ROPE_HALF, dtype=f32) / ROPE_HALF)
    ang = pos.astype(f32)[:, None] * inv
    cos, sin = jnp.cos(ang), jnp.sin(ang)
    d = jnp.arange(HEAD_DIM)
    cosp = jnp.where(d < 2 * ROPE_HALF, cos[:, d % ROPE_HALF], 1.0)
    sap = jnp.where(d < ROPE_HALF, -sin[:, d % ROPE_HALF], 0.0)
    sbp = jnp.where((d >= ROPE_HALF) & (d < 2 * ROPE_HALF), sin[:, d % ROPE_HALF], 0.0)
    one = jnp.ones_like(cosp)
    zero = jnp.zeros_like(cosp)
    return (jnp.concatenate([cosp, cosp, cosp, one], axis=1),
            jnp.concatenate([sap, sap, sap, zero], axis=1),
            jnp.concatenate([sbp, sbp, sbp, zero], axis=1))


def _rope(z, cos, sa, sb):
    outs = []
    for c in range(z.shape[1] // LANES):
        zc = z[:, c * LANES:(c + 1) * LANES]
        outs.append(zc * cos + pltpu.roll(zc, LANES - ROPE_HALF, 1) * sa + pltpu.roll(zc, ROPE_HALF, 1) * sb)
    return outs[0] if len(outs) == 1 else jnp.concatenate(outs, axis=1)


def _proj_kernel(x_ref, w_ref, cos_ref, sa_ref, sb_ref,
                 u_ref, q_ref, k_ref, v_ref, qi_ref, ki_ref, wi_ref, gc_ref, ga_ref):
    x = x_ref[...].astype(bf16)
    cos, sa, sb = cos_ref[:, :LANES], sa_ref[:, :LANES], sb_ref[:, :LANES]
    cos2, sa2, sb2 = cos_ref[:, LANES:], sa_ref[:, LANES:], sb_ref[:, LANES:]

    def mm(c0, c1):
        return jnp.dot(x, w_ref[:, c0:c1], preferred_element_type=f32)

    z = mm(0, 1024)
    u_ref[...] = z[:, :512] * jax.nn.sigmoid(z[:, 512:])
    z = mm(1024, 2048)
    q_ref[...] = (_rope(z, cos, sa, sb) * (HEAD_DIM ** -0.5)).astype(bf16)
    z = mm(2048, 2560)
    k_ref[...] = _rope(z[:, :256], cos, sa, sb)
    v_ref[...] = z[:, 256:]
    z = mm(2560, 3072)
    qi_ref[...] = _rope(z, cos, sa, sb).astype(bf16)
    z = mm(3072, 3200)
    z = _rope(z, cos2, sa2, sb2)
    ki_ref[...] = z[:, :IDX_DIM]
    wi_ref[...] = z[:, IDX_DIM:IDX_DIM + IDX_HEADS]
    gc_ref[...] = jax.nn.sigmoid(mm(3200, 4224))
    ga_ref[...] = jax.nn.sigmoid(mm(4224, 5248))


def _project(x2, w_pad, tabs, tm):
    R, D = x2.shape
    rt = tabs[0].shape[0]
    nt = rt // tm
    row = lambda i: (i, 0)
    tab = lambda i: (i % nt, 0)
    widths = [(512, f32), (1024, bf16), (256, f32), (256, f32), (512, bf16), (IDX_DIM, f32), (IDX_HEADS, f32),
              (1024, f32), (1024, f32)]
    return pl.pallas_call(
        _proj_kernel,
        grid=(R // tm,),
        in_specs=[pl.BlockSpec((tm, D), row),
                  pl.BlockSpec(w_pad.shape, lambda i: (0, 0)),
                  pl.BlockSpec((tm, 256), tab), pl.BlockSpec((tm, 256), tab), pl.BlockSpec((tm, 256), tab)],
        out_specs=[pl.BlockSpec((tm, w), row) for w, _ in widths],
        out_shape=[jax.ShapeDtypeStruct((R, w), dt) for w, dt in widths],
        compiler_params=_cparams(("parallel",)),
        name="proj",
    )(x2, w_pad, *tabs)


def _tile_lanes(t, width):
    return t if width == LANES else jnp.concatenate([t] * (width // LANES), axis=1)


def _select_tau(S, n_ch, k, n_adm):
    _, RW, CH = S.shape
    kf = float(k)
    inf = float("inf")
    rep = lambda a: jnp.broadcast_to(a, (RW, LANES))
    ones = jnp.ones((LANES, LANES), bf16)

    def mm_body(c, carry):
        mn, mx = carry
        s = S[c]
        mx = jnp.maximum(mx, jnp.max(s, axis=1, keepdims=True))
        mn = jnp.minimum(mn, jnp.min(jnp.where(s == -inf, inf, s), axis=1, keepdims=True))
        return mn, mx

    mn, mx = lax.fori_loop(0, n_ch, mm_body, (jnp.full((RW, 1), inf, f32), jnp.full((RW, 1), -inf, f32)))
    hi0 = mx + jnp.maximum(jnp.abs(mx), 1e-30) * (2.0 ** -20)

    def count(tb, strict):
        def body(c, acc):
            s = S[c]
            for l in range(CH // LANES):
                sl = s[:, l * LANES:(l + 1) * LANES]
                hit = (sl > tb) if strict else (sl >= tb)
                acc = acc + jnp.where(hit, 1.0, 0.0)
            return acc

        acc = lax.fori_loop(0, n_ch, body, jnp.zeros((RW, LANES), f32))
        return jnp.dot(acc.astype(bf16), ones, preferred_element_type=f32)

    n_adm_b = rep(n_adm)
    done0 = jnp.where(n_adm_b <= kf, 1.0, 0.0)
    tau0 = jnp.full((RW, LANES), -3e38, f32)
    zero = jnp.zeros((RW, LANES), f32)

    def cond(st):
        return jnp.logical_and(jnp.min(st[5]) < 0.5, st[7] < 400)

    def body(st):
        lo, hi, cnt_lo, cnt_hi, tau, done, tie, it = st
        half = lo + (hi - lo) * 0.5
        conv = (half <= lo) | (half >= hi)
        frac = jnp.clip((cnt_lo - kf + 0.5) / (cnt_lo - cnt_hi), 0.02, 0.98)
        interp = lo + (hi - lo) * frac
        use_half = ((it % 2) == 1) | (interp <= lo) | (interp >= hi)
        mid = jnp.where(use_half, half, interp)
        cnt = count(mid, False)
        exact = cnt == kf
        ge = cnt >= kf
        was = done > 0.5
        tau_n = jnp.where(was, tau, jnp.where(exact, mid, lo))
        tie_n = jnp.where(was, tie, jnp.where(conv & (~exact) & (cnt_lo > kf), 1.0, 0.0))
        done_n = jnp.where(was | exact | conv, 1.0, 0.0)
        return (jnp.where(ge, mid, lo), jnp.where(ge, hi, mid), jnp.where(ge, cnt, cnt_lo),
                jnp.where(ge, cnt_hi, cnt), tau_n, done_n, tie_n, it + 1)

    st = lax.while_loop(cond, body, (rep(mn), rep(hi0), n_adm_b, zero, tau0, done0, zero, jnp.int32(0)))
    tau, tie = st[4], st[6]

    @pl.when(jnp.max(tie) > 0.5)
    def _():
        tau1, tie1 = tau[:, 0:1], tie[:, 0:1]
        need = kf - count(tau, True)[:, 0:1]
        r = lax.broadcasted_iota(jnp.int32, (CH, CH), 0)
        c_ = lax.broadcasted_iota(jnp.int32, (CH, CH), 1)
        tri = jnp.where(r <= c_, 1.0, 0.0).astype(bf16)

        def fix(c, run):
            s = S[c]
            eq = (s == tau1) & (tie1 > 0.5)
            eqf = jnp.where(eq, 1.0, 0.0)
            pref = jnp.dot(eqf.astype(bf16), tri, preferred_element_type=f32) + run
            S[c] = jnp.where(eq & (pref > need), -inf, s)
            return run + jnp.sum(eqf, axis=1, keepdims=True)

        lax.fori_loop(0, n_ch, fix, jnp.zeros((RW, 1), f32))

    return tau


def _attn_prompt_kernel(qi_ref, wi_ref, kit_ref, q_ref, kt_ref, v_ref, o_ref, S, *, topk):
    qb = pl.program_id(1)
    QB = q_ref.shape[1]
    CH = S.shape[2]
    n_ch = (qb * QB + QB + CH - 1) // CH
    qpos = qb * QB + lax.broadcasted_iota(jnp.int32, (QB, 1), 0)
    wsc = wi_ref[0] * ((IDX_HEADS * IDX_DIM) ** -0.5)
    qi = jnp.concatenate([qi_ref[0, :, h * IDX_DIM:(h + 1) * IDX_DIM] for h in range(IDX_HEADS)], axis=0)

    def score_chunk(c, carry):
        s = jnp.dot(qi, kit_ref[0, c], preferred_element_type=f32)
        s = jnp.maximum(s, 0.0).reshape(IDX_HEADS, QB, CH)
        sc = s[0] * wsc[:, 0:1]
        for h in range(1, IDX_HEADS):
            sc = sc + s[h] * wsc[:, h:h + 1]
        kpos = c * CH + lax.broadcasted_iota(jnp.int32, (1, CH), 1)
        S[c] = jnp.where(kpos <= qpos, sc, -float("inf"))
        return carry

    lax.fori_loop(0, n_ch, score_chunk, 0)
    tau = _select_tau(S, n_ch, topk, (qpos + 1).astype(f32))

    G = N_HEADS // N_KV_HEADS
    taub = _tile_lanes(tau, CH)
    qgs = [jnp.concatenate([q_ref[0, :, (g * G + j) * HEAD_DIM:(g * G + j + 1) * HEAD_DIM] for j in range(G)],
                           axis=0) for g in range(N_KV_HEADS)]

    def chunk(c, carry):
        bias = jnp.where(S[c] >= taub, 0.0, NEG)[None]
        off = pl.multiple_of(c * CH, CH)
        out = []
        for g in range(N_KV_HEADS):
            m, acc = carry[g]
            s = jnp.dot(qgs[g], kt_ref[0, c, g], preferred_element_type=f32)
            s = (s.reshape(G, QB, CH) + bias).reshape(G * QB, CH)
            m_new = jnp.maximum(m, jnp.max(s, axis=1, keepdims=True))
            p = jnp.exp(s - m_new).astype(bf16)
            acc = jnp.exp(m - m_new) * acc + jnp.dot(p, v_ref[0, g, pl.ds(off, CH), :],
                                                      preferred_element_type=f32)
            out.append((m_new, acc))
        return tuple(out)

    init = tuple((jnp.full((G * QB, 1), NEG, f32), jnp.zeros((G * QB, LANES), f32)) for _ in range(N_KV_HEADS))
    res = lax.fori_loop(0, n_ch, chunk, init)
    for g in range(N_KV_HEADS):
        acc = res[g][1]
        o = acc[:, :HEAD_DIM] / acc[:, HEAD_DIM:HEAD_DIM + 1]
        for j in range(G):
            h = g * G + j
            o_ref[0, :, h * HEAD_DIM:(h + 1) * HEAD_DIM] = o[j * QB:(j + 1) * QB]


def _attn_prompt(q, k, v, qi, ki, wi):
    N, T, D = q.shape
    QB = min(Q_ROWS, T)
    topk = min(TOPK_MAX, T // 4)
    CH = min(KEY_CHUNK, T)
    NC = T // CH
    kit = ki.astype(bf16).reshape(N, NC, CH, IDX_DIM).transpose(0, 1, 3, 2)
    kt = k.astype(bf16).reshape(N, NC, CH, N_KV_HEADS, HEAD_DIM).transpose(0, 1, 3, 4, 2)
    vh = v.astype(bf16).reshape(N, T, N_KV_HEADS, HEAD_DIM).transpose(0, 2, 1, 3)
    vh = jnp.concatenate([vh, jnp.ones(vh.shape[:3] + (1,), bf16),
                          jnp.zeros(vh.shape[:3] + (LANES - HEAD_DIM - 1,), bf16)], axis=3)
    return pl.pallas_call(
        functools.partial(_attn_prompt_kernel, topk=topk),
        grid=(N, T // QB),
        in_specs=[pl.BlockSpec((1, QB, IDX_HEADS * IDX_DIM), lambda n, b: (n, b, 0)),
                  pl.BlockSpec((1, QB, IDX_HEADS), lambda n, b: (n, b, 0)),
                  pl.BlockSpec((1, NC, IDX_DIM, CH), lambda n, b: (n, 0, 0, 0)),
                  pl.BlockSpec((1, QB, D), lambda n, b: (n, b, 0)),
                  pl.BlockSpec((1, NC, N_KV_HEADS, HEAD_DIM, CH), lambda n, b: (n, 0, 0, 0, 0)),
                  pl.BlockSpec((1, N_KV_HEADS, T, LANES), lambda n, b: (n, 0, 0, 0))],
        out_specs=pl.BlockSpec((1, QB, D), lambda n, b: (n, b, 0)),
        out_shape=jax.ShapeDtypeStruct((N, T, D), f32),
        scratch_shapes=[pltpu.VMEM((NC, QB, CH), f32)],
        compiler_params=_cparams(("parallel", "arbitrary")),
        name="attn_prompt",
    )(qi, wi, kit, q, kt, vh)


def _sample_scores_kernel(pt_ref, qi_ref, w_ref, kin_ref, *rest):
    pages = rest[:PAGES_PER_STEP]
    sp_ref, sn_ref = rest[PAGES_PER_STEP:]
    TQ = sp_ref.shape[2]
    qi = qi_ref[0]
    w = w_ref[0] * ((IDX_HEADS * IDX_DIM) ** -0.5)
    nt = (((1,), (1,)), ((), ()))

    def scores(keys):
        s = lax.dot_general(qi, keys, nt, preferred_element_type=f32)
        s = jnp.maximum(s, 0.0) * w
        return jnp.sum(s.reshape(TQ, IDX_HEADS, s.shape[1]), axis=1)

    keys = jnp.concatenate([p[0].astype(bf16) for p in pages], axis=0)
    sp_ref[0, 0] = scores(keys)

    @pl.when(pl.program_id(1) == 0)
    def _():
        s = scores(kin_ref[0])
        t = lax.broadcasted_iota(jnp.int32, s.shape, 0)
        j = lax.broadcasted_iota(jnp.int32, s.shape, 1)
        sn_ref[0] = jnp.where(j <= t, s, -float("inf"))


def _sample_select_kernel(sp_ref, sn_ref, bias_ref, S, *, topk, past, tq):
    NCP = sp_ref.shape[0]
    RW, CH = S.shape[1], S.shape[2]
    for c in range(NCP):
        S[c] = sp_ref[c]
    S[NCP] = jnp.concatenate([sn_ref[...], jnp.full((RW, CH - LANES), -float("inf"), f32)], axis=1)
    t = lax.broadcasted_iota(jnp.int32, (RW, 1), 0) % tq
    taub = _tile_lanes(_select_tau(S, NCP + 1, topk, (past + 1 + t).astype(f32)), CH)
    for c in range(NCP + 1):
        bias_ref[c] = jnp.where(S[c] >= taub, 0.0, NEG)


def _sample_attend_kernel(pt_ref, q_ref, bp_ref, bn_ref, kn_ref, vn_ref, *rest):
    kp = rest[:PAGES_PER_STEP]
    vp = rest[PAGES_PER_STEP:2 * PAGES_PER_STEP]
    o_ref, m_ref, l_ref, acc_ref = rest[2 * PAGES_PER_STEP:]
    p_id = pl.program_id(1)
    q = q_ref[0]
    R = q.shape[0]
    TQ = bp_ref.shape[2]
    nt = (((1,), (1,)), ((), ()))

    @pl.when(p_id == 0)
    def _():
        m_ref[...] = jnp.full(m_ref.shape, NEG, f32)
        l_ref[...] = jnp.zeros(l_ref.shape, f32)
        acc_ref[...] = jnp.zeros(acc_ref.shape, f32)

    def step(keys, vals, bias):
        s = lax.dot_general(q, keys, nt, preferred_element_type=f32)
        L = s.shape[1]
        b = jnp.broadcast_to(bias[:, None, :], (TQ, R // TQ, L)).reshape(R, L)
        s = jnp.where(b < 0.0, NEG, s)
        m = m_ref[...]
        m_new = jnp.maximum(m, jnp.max(s, axis=1, keepdims=True))
        p = jnp.exp(s - m_new)
        a = jnp.exp(m - m_new)
        l_ref[...] = a * l_ref[...] + jnp.sum(p, axis=1, keepdims=True)
        acc_ref[...] = a * acc_ref[...] + jnp.dot(p.astype(bf16), vals, preferred_element_type=f32)
        m_ref[...] = m_new

    keys = jnp.concatenate([p[0].astype(bf16) for p in kp], axis=0)
    vals = jnp.concatenate([p[0].astype(bf16) for p in vp], axis=0)
    step(keys, vals, bp_ref[0, 0])

    @pl.when(p_id == pl.num_programs(1) - 1)
    def _():
        step(kn_ref[0], vn_ref[0], bn_ref[0])
        o_ref[0] = acc_ref[...] / l_ref[...]


def _attn_sample(q, k_new, v_new, qi, ki_new, wi, cache_k, cache_v, cache_kidx, page_table):
    N, tq, _ = q.shape
    n_pages = page_table.shape[1]
    past = n_pages * PAGE
    topk = min(TOPK_MAX, (past + tq) // 4)
    PP = PAGES_PER_STEP
    NP = n_pages // PP
    LP = PP * PAGE
    n_pool = cache_k.shape[0]
    KV = N_KV_HEADS * HEAD_DIM

    qi2 = qi.reshape(N, tq * IDX_HEADS, IDX_DIM)
    w2 = wi.reshape(N, tq * IDX_HEADS, 1)
    pad_rows = lambda a: jnp.pad(a, ((0, 0), (0, LANES - tq), (0, 0)))
    kin = pad_rows(ki_new.astype(bf16))

    def page_spec(shape, j):
        return pl.BlockSpec(shape, lambda n, p, pt: (pt[n, p * PP + j], 0, 0))

    sp, sn = pl.pallas_call(
        _sample_scores_kernel,
        grid_spec=pltpu.PrefetchScalarGridSpec(
            num_scalar_prefetch=1, grid=(N, NP),
            in_specs=[pl.BlockSpec((1, tq * IDX_HEADS, IDX_DIM), lambda n, p, pt: (n, 0, 0)),
                      pl.BlockSpec((1, tq * IDX_HEADS, 1), lambda n, p, pt: (n, 0, 0)),
                      pl.BlockSpec((1, LANES, IDX_DIM), lambda n, p, pt: (n, 0, 0))]
                     + [page_spec((1, PAGE, IDX_DIM), j) for j in range(PP)],
            out_specs=[pl.BlockSpec((1, 1, tq, LP), lambda n, p, pt: (n, p, 0, 0)),
                       pl.BlockSpec((1, tq, LANES), lambda n, p, pt: (n, 0, 0))]),
        out_shape=[jax.ShapeDtypeStruct((N, NP, tq, LP), f32), jax.ShapeDtypeStruct((N, tq, LANES), f32)],
        compiler_params=_cparams(("parallel", "arbitrary")),
        name="sample_scores",
    )(page_table, qi2, w2, kin, *([cache_kidx] * PP))

    CH = KEY_CHUNK
    NCP = past // CH
    RW = N * tq
    sp2 = sp.transpose(0, 2, 1, 3).reshape(RW, NCP, CH).transpose(1, 0, 2)
    bias = pl.pallas_call(
        functools.partial(_sample_select_kernel, topk=topk, past=past, tq=tq),
        out_shape=jax.ShapeDtypeStruct((NCP + 1, RW, CH), f32),
        scratch_shapes=[pltpu.VMEM((NCP + 1, RW, CH), f32)],
        compiler_params=pltpu.CompilerParams(vmem_limit_bytes=VMEM_LIMIT),
        name="sample_select",
    )(sp2, sn.reshape(RW, LANES))
    bp = bias[:NCP].transpose(1, 0, 2).reshape(N, tq, NP, LP).transpose(0, 2, 1, 3)
    bn = bias[NCP, :, :LANES].reshape(N, tq, LANES)

    G = N_HEADS // N_KV_HEADS
    q5 = q.reshape(N, tq, N_KV_HEADS, G, 1, HEAD_DIM)
    eye = jnp.eye(N_KV_HEADS, dtype=q.dtype).reshape(1, 1, N_KV_HEADS, 1, N_KV_HEADS, 1)
    qbd = (q5 * eye).reshape(N, tq * N_HEADS, KV)
    kn = pad_rows(k_new.astype(bf16))
    vn = pad_rows(v_new.astype(bf16))
    ck = cache_k.reshape(n_pool, PAGE, KV)
    cv = cache_v.reshape(n_pool, PAGE, KV)
    R = tq * N_HEADS
    o = pl.pallas_call(
        _sample_attend_kernel,
        grid_spec=pltpu.PrefetchScalarGridSpec(
            num_scalar_prefetch=1, grid=(N, NP),
            in_specs=[pl.BlockSpec((1, R, KV), lambda n, p, pt: (n, 0, 0)),
                      pl.BlockSpec((1, 1, tq, LP), lambda n, p, pt: (n, p, 0, 0)),
                      pl.BlockSpec((1, tq, LANES), lambda n, p, pt: (n, 0, 0)),
                      pl.BlockSpec((1, LANES, KV), lambda n, p, pt: (n, 0, 0)),
                      pl.BlockSpec((1, LANES, KV), lambda n, p, pt: (n, 0, 0))]
                     + [page_spec((1, PAGE, KV), j) for j in range(PP)]
                     + [page_spec((1, PAGE, KV), j) for j in range(PP)],
            out_specs=pl.BlockSpec((1, R, KV), lambda n, p, pt: (n, 0, 0)),
            scratch_shapes=[pltpu.VMEM((R, 1), f32), pltpu.VMEM((R, 1), f32), pltpu.VMEM((R, KV), f32)]),
        out_shape=jax.ShapeDtypeStruct((N, R, KV), f32),
        compiler_params=_cparams(("parallel", "arbitrary")),
        name="sample_attend",
    )(page_table, qbd, bp, bn, kn, vn, *([ck] * PP), *([cv] * PP))
    o6 = o.reshape(N, tq, N_KV_HEADS, G, N_KV_HEADS, HEAD_DIM)
    sel = jnp.eye(N_KV_HEADS, dtype=f32).reshape(1, 1, N_KV_HEADS, 1, N_KV_HEADS, 1)
    return jnp.sum(o6 * sel, axis=4).reshape(N, tq, N_HEADS * HEAD_DIM)


HALO = 32


def _conv_kernel(u_ref, prev_ref, cw_ref, cb_ref, g_ref, b_ref, wo_ref, o_ref, buf):
    tt = u_ref.shape[1]

    @pl.when(pl.program_id(1) == 0)
    def _():
        buf[0:HALO] = prev_ref[0]

    @pl.when(pl.program_id(1) > 0)
    def _():
        buf[0:HALO] = buf[tt:tt + HALO]

    buf[HALO:HALO + tt] = u_ref[0]
    off = HALO - (D_CONV_W - 1)
    y = buf[off:off + tt] * cw_ref[0:1, :]
    for j in range(1, D_CONV_W):
        y = y + buf[off + j:off + j + tt] * cw_ref[j:j + 1, :]
    y = y + cb_ref[...]
    mu = jnp.mean(y, axis=-1, keepdims=True)
    yc = y - mu
    var = jnp.mean(yc * yc, axis=-1, keepdims=True)
    y = yc * lax.rsqrt(var + LN_EPS) * g_ref[...] + b_ref[...]
    y = y * jax.nn.sigmoid(y)
    o_ref[0] = jnp.dot(y.astype(bf16), wo_ref[...], preferred_element_type=f32)


def _conv_module(u, prev, conv_w, conv_b, ln_g, ln_b, w_conv_out_bf, tt):
    N, T, C = u.shape
    D = w_conv_out_bf.shape[1]
    prev_pad = jnp.pad(prev, ((0, 0), (HALO - prev.shape[1], 0), (0, 0)))
    cw = jnp.pad(conv_w, ((0, 32 - conv_w.shape[0]), (0, 0)))
    row2 = lambda a: a.reshape(1, -1)
    return pl.pallas_call(
        _conv_kernel,
        grid=(N, T // tt),
        in_specs=[pl.BlockSpec((1, tt, C), lambda n, t: (n, t, 0)),
                  pl.BlockSpec((1, HALO, C), lambda n, t: (n, 0, 0)),
                  pl.BlockSpec((32, C), lambda n, t: (0, 0)),
                  pl.BlockSpec((1, C), lambda n, t: (0, 0)),
                  pl.BlockSpec((1, C), lambda n, t: (0, 0)),
                  pl.BlockSpec((1, C), lambda n, t: (0, 0)),
                  pl.BlockSpec((C, D), lambda n, t: (0, 0))],
        out_specs=pl.BlockSpec((1, tt, D), lambda n, t: (n, t, 0)),
        out_shape=jax.ShapeDtypeStruct((N, T, D), f32),
        scratch_shapes=[pltpu.VMEM((HALO + tt + 8, C), f32)],
        compiler_params=_cparams(("parallel", "arbitrary")),
        name="conv_module",
    )(u, prev_pad, cw, row2(conv_b), row2(ln_g), row2(ln_b), w_conv_out_bf)


def _merge_kernel(x_ref, co_ref, ao_ref, gc_ref, ga_ref, wo_ref, g_ref, b_ref, ht_ref, htb_ref):
    m = gc_ref[...] * co_ref[...] + ga_ref[...] * ao_ref[...]
    z = ALPHA * x_ref[...] + jnp.dot(m.astype(bf16), wo_ref[...], preferred_element_type=f32)
    mu = jnp.mean(z, axis=-1, keepdims=True)
    zc = z - mu
    var = jnp.mean(zc * zc, axis=-1, keepdims=True)
    ht = (zc * lax.rsqrt(var + LN_EPS) * g_ref[...] + b_ref[...]).T
    ht_ref[...] = ht
    htb_ref[...] = ht.astype(bf16)


def _merge(x2, conv_o, attn_o, sgc, sga, w_out_bf, ln_g, ln_b, tm):
    R, D = x2.shape
    row = lambda i: (i, 0)
    cst = lambda i: (0, 0)
    col = lambda i: (0, i)
    return pl.pallas_call(
        _merge_kernel,
        grid=(R // tm,),
        in_specs=[pl.BlockSpec((tm, D), row)] * 5
                 + [pl.BlockSpec((D, D), cst), pl.BlockSpec((1, D), cst), pl.BlockSpec((1, D), cst)],
        out_specs=[pl.BlockSpec((D, tm), col), pl.BlockSpec((D, tm), col)],
        out_shape=[jax.ShapeDtypeStruct((D, R), f32), jax.ShapeDtypeStruct((D, R), bf16)],
        compiler_params=_cparams(("parallel",)),
        name="merge",
    )(x2, conv_o, attn_o, sgc, sga, w_out_bf, ln_g.reshape(1, D), ln_b.reshape(1, D))


def _top16(s):
    K, Tn = s.shape
    rows = lax.broadcasted_iota(jnp.int32, (K, Tn), 0).astype(f32)
    rank = jnp.full((K, Tn), float(PEER_TOPK), f32)
    tops = []
    for r in range(PEER_TOPK):
        m = jnp.max(s, axis=0, keepdims=True)
        first = jnp.min(jnp.where(s == m, rows, float(K)), axis=0, keepdims=True)
        pick = rows == first
        rank = jnp.where(pick, float(r), rank)
        s = jnp.where(pick, -float("inf"), s)
        tops.append(m)
    return rank, jnp.concatenate(tops, axis=0)


def _route_kernel(ht_ref, wq_ref, k1_ref, k2_ref, r2_ref, e2_ref, c_ref, e1_ref):
    Tn = ht_ref.shape[1]
    qh = jnp.dot(wq_ref[...], ht_ref[...], preferred_element_type=f32)
    half = PEER_DQ // 2
    T16 = PEER_TOPK
    ninf = -float("inf")
    sub = lax.broadcasted_iota(jnp.int32, (8, Tn), 0)
    subf = sub.astype(f32)
    for h in range(PEER_HEADS):
        q1 = qh[h * PEER_DQ:h * PEER_DQ + half].astype(bf16)
        q2 = qh[h * PEER_DQ + half:(h + 1) * PEER_DQ].astype(bf16)
        s1 = jnp.dot(k1_ref[h], q1, preferred_element_type=f32)
        s2 = jnp.dot(k2_ref[h], q2, preferred_element_type=f32)
        r1, t1 = _top16(s1)
        r2, t2 = _top16(s2)
        limits = [16, 8, 5, 4, 3, 2, 2, 2]
        cands, idxs = [], []
        cands.append(t1[0:1] + t2[0:8]); idxs.append(subf)
        cands.append(t1[0:1] + t2[8:16]); idxs.append(subf + 8.0)
        for a in range(1, 8):
            cands.append(jnp.where(sub < limits[a], t1[a:a + 1] + t2[0:8], ninf))
            idxs.append(subf + float(16 * a))
        cands.append(t1[8:16] + t2[0:1]); idxs.append((subf + 8.0) * 16.0)
        cand = jnp.concatenate(cands, axis=0)
        cidx = jnp.concatenate(idxs, axis=0)
        cmax = cand[0:1]
        ecand = jnp.exp(cand - cmax)
        picked = jnp.zeros(cand.shape, f32)
        for _ in range(T16):
            m = jnp.max(cand, axis=0, keepdims=True)
            first = jnp.min(jnp.where(cand == m, cidx, 4096.0), axis=0, keepdims=True)
            pick = cidx == first
            picked = jnp.where(pick, 1.0, picked)
            cand = jnp.where(pick, ninf, cand)
        z = jnp.sum(picked * ecand, axis=0, keepdims=True)
        la = [jnp.sum(picked[0:16], axis=0, keepdims=True)]
        for a in range(1, 8):
            la.append(jnp.sum(picked[8 + 8 * a:16 + 8 * a], axis=0, keepdims=True))
        ltail = picked[72:80]
        c = jnp.zeros(s1.shape, f32)
        for a in range(8):
            c = jnp.where(r1 == float(a), la[a], c)
        for a in range(8, 16):
            c = jnp.where(r1 == float(a), ltail[a - 8:a - 7], c)
        e1 = jnp.where(r1 < float(T16), jnp.exp(s1 - t1[0:1]), 0.0) / z
        e2 = jnp.where(r2 < float(T16), jnp.exp(s2 - t2[0:1]), 0.0)
        r2_ref[h] = r2.astype(bf16)
        e2_ref[h] = e2.astype(bf16)
        c_ref[h] = c
        e1_ref[h] = e1


def _route(ht_bf, wqt_bf, k1_bf, k2_bf, tn):
    D, R = ht_bf.shape
    shp = lambda dt: jax.ShapeDtypeStruct((PEER_HEADS, PEER_NKEYS, R), dt)
    ospec = pl.BlockSpec((PEER_HEADS, PEER_NKEYS, tn), lambda i: (0, 0, i))
    return pl.pallas_call(
        _route_kernel,
        grid=(R // tn,),
        in_specs=[pl.BlockSpec((D, tn), lambda i: (0, i)),
                  pl.BlockSpec(wqt_bf.shape, lambda i: (0, 0)),
                  pl.BlockSpec(k1_bf.shape, lambda i: (0, 0, 0)),
                  pl.BlockSpec(k2_bf.shape, lambda i: (0, 0, 0))],
        out_specs=[ospec] * 4,
        out_shape=[shp(bf16), shp(bf16), shp(f32), shp(f32)],
        compiler_params=_cparams(("parallel",)),
        name="peer_route",
    )(ht_bf, wqt_bf, k1_bf, k2_bf)


ET = 1024
PEER_SUBTILE = 256


def _peer_kernel(htb_ref, ht_ref, u_ref, vt_ref, r2_ref, e2_ref, c_ref, e1_ref, g_ref, b_ref,
                 y_ref, acc_ref, p_ref):
    et = pl.program_id(1)
    NK = PEER_NKEYS

    @pl.when(et == 0)
    def _():
        acc_ref[...] = jnp.zeros(acc_ref.shape, f32)

    Tn = htb_ref.shape[1]
    SL = 16
    TS = p_ref.shape[2]
    zero = jnp.zeros((), bf16)
    acts = [[jnp.dot(u_ref[ii * NK:(ii + 1) * NK, :], htb_ref[:, ts * TS:(ts + 1) * TS],
                     preferred_element_type=f32) for ii in range(ET // NK)] for ts in range(Tn // TS)]
    for ts in range(Tn // TS):
        tok = slice(ts * TS, (ts + 1) * TS)
        for ii in range(ET // NK):
            a = acts[ts][ii]
            gl = (0.5 * a * (1.0 + lax.erf(a * (2.0 ** -0.5)))).astype(bf16)
            w = [jnp.zeros((SL, TS), bf16) for _ in range(NK // SL)]
            for h in range(PEER_HEADS):
                cb = jnp.broadcast_to(c_ref[h, ii:ii + 1, tok], (SL, TS)).astype(bf16)
                eb = jnp.broadcast_to(e1_ref[h, ii:ii + 1, tok], (SL, TS)).astype(bf16)
                for s in range(NK // SL):
                    rows = slice(s * SL, (s + 1) * SL)
                    w[s] = w[s] + jnp.where(r2_ref[h, rows, tok] < cb, eb * e2_ref[h, rows, tok], zero)
            for s in range(NK // SL):
                p_ref[ts, ii * NK + s * SL:ii * NK + (s + 1) * SL, :] = w[s] * gl[s * SL:(s + 1) * SL]
        acc_ref[:, tok] += jnp.dot(vt_ref[...], p_ref[ts], preferred_element_type=f32)

    @pl.when(et == pl.num_programs(1) - 1)
    def _():
        z = ALPHA * ht_ref[...] + acc_ref[...]
        mu = jnp.mean(z, axis=0, keepdims=True)
        zc = z - mu
        var = jnp.mean(zc * zc, axis=0, keepdims=True)
        y_ref[...] = (zc * lax.rsqrt(var + LN_EPS) * g_ref[...] + b_ref[...]).T


def _peer_dense(ht_bf, ht, u_bf, vt_bf, r2, e2, c, e1, ln_g, ln_b, tn):
    D, R = ht.shape
    E = u_bf.shape[0]
    NI = ET // PEER_NKEYS
    ts = min(tn, PEER_SUBTILE)
    tok = lambda i, e: (0, i)
    rt = pl.BlockSpec((PEER_HEADS, PEER_NKEYS, tn), lambda i, e: (0, 0, i))
    ri = pl.BlockSpec((PEER_HEADS, NI, tn), lambda i, e: (0, e, i))
    return pl.pallas_call(
        _peer_kernel,
        grid=(R // tn, E // ET),
        in_specs=[pl.BlockSpec((D, tn), tok), pl.BlockSpec((D, tn), tok),
                  pl.BlockSpec((ET, D), lambda i, e: (e, 0)),
                  pl.BlockSpec((D, ET), lambda i, e: (0, e)),
                  rt, rt, ri, ri,
                  pl.BlockSpec((D, 1), lambda i, e: (0, 0)), pl.BlockSpec((D, 1), lambda i, e: (0, 0))],
        out_specs=pl.BlockSpec((tn, D), lambda i, e: (i, 0)),
        out_shape=jax.ShapeDtypeStruct((R, D), f32),
        scratch_shapes=[pltpu.VMEM((D, tn), f32), pltpu.VMEM((tn // ts, ET, ts), bf16)],
        compiler_params=_cparams(("parallel", "arbitrary")),
        name="peer_dense",
    )(ht_bf, ht, u_bf, vt_bf, r2, e2, c, e1, ln_g.reshape(D, 1), ln_b.reshape(D, 1))


def _channel_mix(ht, ht_bf, wqt_bf, k1_bf, k2_bf, u_bf, vt_bf, ln_g, ln_b, tn_route, tn_dense):
    r2, e2, c, e1 = _route(ht_bf, wqt_bf, k1_bf, k2_bf, tn_route)
    return _peer_dense(ht_bf, ht, u_bf, vt_bf, r2, e2, c, e1, ln_g, ln_b, tn_dense)


def _prep_weights(w_in, w_conv_out, w_out, peer_w_query, peer_keys1, peer_keys2, peer_u, peer_v):
    D = w_in.shape[0]
    split = 2 * (D // 2) + N_HEADS * HEAD_DIM + 2 * N_KV_HEADS * HEAD_DIM + IDX_HEADS * IDX_DIM + IDX_DIM + IDX_HEADS
    pad = (-split) % LANES
    w_pad = jnp.concatenate([w_in[:, :split], jnp.zeros((D, pad), w_in.dtype), w_in[:, split:]], axis=1).astype(bf16)
    wqt = peer_w_query.reshape(D, PEER_HEADS * PEER_DQ).T.astype(bf16)
    return dict(w_pad=w_pad, wco=w_conv_out.astype(bf16), wo=w_out.astype(bf16), wqt=wqt,
                k1=peer_keys1.astype(bf16), k2=peer_keys2.astype(bf16),
                u=peer_u.astype(bf16), vt=peer_v.T.astype(bf16))


def _group(x, pos, prev, W, conv_w, conv_b, conv_ln_g, conv_ln_b, ln1_g, ln1_b, ln2_g, ln2_b,
           attn_fn, tm, tt, tn_route, tn_dense):
    N, T, D = x.shape
    R = N * T
    x2 = x.reshape(R, D)
    tabs = _rope_tables(pos if T % tm == 0 else jnp.tile(pos, tm // T))
    u, q, k, v, qi, ki, wi, sgc, sga = _project(x2, W["w_pad"], tabs, tm)
    r3 = lambda a: a.reshape(N, T, a.shape[-1])
    attn_o = attn_fn(r3(q), r3(k), r3(v), r3(qi), r3(ki), r3(wi))
    u3 = r3(u)
    if T % tt == 0:
        conv_o = _conv_module(u3, prev, conv_w, conv_b, conv_ln_g, conv_ln_b, W["wco"], tt)
    else:
        up = jnp.pad(u3, ((0, 0), (0, tt - T), (0, 0)))
        conv_o = _conv_module(up, prev, conv_w, conv_b, conv_ln_g, conv_ln_b, W["wco"], tt)[:, :T]
    ht, ht_bf = _merge(x2, conv_o.reshape(R, D), attn_o.reshape(R, D), sgc, sga, W["wo"], ln1_g, ln1_b, tm)
    y2 = _channel_mix(ht, ht_bf, W["wqt"], W["k1"], W["k2"], W["u"], W["vt"], ln2_g, ln2_b, tn_route, tn_dense)
    return y2.reshape(N, T, D), k, v, ki, u3


def kernel(x_prompt, x_sample, cache_k, cache_v, cache_kidx, state_conv, page_table, w_in, conv_w, conv_b,
           conv_ln_g, conv_ln_b, w_conv_out, w_out, ln1_g, ln1_b, peer_w_query, peer_keys1, peer_keys2,
           peer_u, peer_v, ln2_g, ln2_b):
    W = _prep_weights(w_in, w_conv_out, w_out, peer_w_query, peer_keys1, peer_keys2, peer_u, peer_v)
    common = (W, conv_w, conv_b, conv_ln_g, conv_ln_b, ln1_g, ln1_b, ln2_g, ln2_b)
    C = conv_w.shape[1]
    keep = D_CONV_W - 1

    N, T, D = x_prompt.shape
    tm = min(256, N * T)
    yp, k_p, v_p, kidx_p, u_p = _group(
        x_prompt, jnp.arange(T, dtype=jnp.int32), jnp.zeros((N, keep, C), f32), *common,
        _attn_prompt, tm, min(512, T), min(256, N * T), min(512, N * T))
    conv_p = u_p[:, T - keep:]

    NS, tq, _ = x_sample.shape
    past = page_table.shape[1] * PAGE
    attn_s = functools.partial(_attn_sample, cache_k=cache_k, cache_v=cache_v, cache_kidx=cache_kidx,
                               page_table=page_table)
    rs = NS * tq
    ys, k_s, v_s, kidx_s, u_s = _group(
        x_sample, past + jnp.arange(tq, dtype=jnp.int32), state_conv, *common,
        lambda q, k, v, qi, ki, wi: attn_s(q, k, v, qi, ki, wi), rs, 8, rs, rs)
    conv_s = jnp.concatenate([state_conv, u_s], axis=1)[:, -keep:]

    kv4 = lambda a, n, t: a.reshape(n, t, N_KV_HEADS, HEAD_DIM)
    return (yp, ys, kv4(k_p, N, T), kv4(v_p, N, T), kidx_p.reshape(N, T, IDX_DIM), conv_p,
            kv4(k_s, NS, tq), kv4(v_s, NS, tq), kidx_s.reshape(NS, tq, IDX_DIM), conv_s)
```

```python
import functools
import math

import jax
import jax.numpy as jnp
from jax import lax
from jax.experimental import pallas as pl
from jax.experimental.pallas import tpu as pltpu

f32 = jnp.float32
bf16 = jnp.bfloat16

D_CONV_W = 31
N_HEADS = 16
N_KV_HEADS = 4
HEAD_DIM = 64
IDX_HEADS = 8
IDX_DIM = 64
TOPK_MAX = 256
ROPE_THETA = 500000.0
ROPE_HALF = 8
PAGE = 128
PEER_HEADS = 8
PEER_NKEYS = 128
PEER_DQ = 128
PEER_TOPK = 16
ALPHA = 2.0 ** 0.25
LN_EPS = 1e-5

LANES = 128
VMEM_LIMIT = 56 * 1024 * 1024
NEG = -1e30
KEY_CHUNK = 512
Q_ROWS = 256
PAGES_PER_STEP = 16


def _cparams(sem):
    return pltpu.CompilerParams(dimension_semantics=sem, vmem_limit_bytes=VMEM_LIMIT)


def _rope_tables(pos):
    inv = jnp.power(ROPE_THETA, -jnp.arange(ROPE_HALF, dtype=f32) / ROPE_HALF)
    ang = pos.astype(f32)[:, None] * inv
    cos, sin = jnp.cos(ang), jnp.sin(ang)
    d = jnp.arange(HEAD_DIM)
    cosp = jnp.where(d < 2 * ROPE_HALF, cos[:, d % ROPE_HALF], 1.0)
    sap = jnp.where(d < ROPE_HALF, -sin[:, d % ROPE_HALF], 0.0)
    sbp = jnp.where((d >= ROPE_HALF) & (d < 2 * ROPE_HALF), sin[:, d % ROPE_HALF], 0.0)
    one = jnp.ones_like(cosp)
    zero = jnp.zeros_like(cosp)
    return (jnp.concatenate([cosp, cosp, cosp, one], axis=1),
            jnp.concatenate([sap, sap, sap, zero], axis=1),
            jnp.concatenate([sbp, sbp, sbp, zero], axis=1))


def _rope(z, cos, sa, sb):
    outs = []
    for c in range(z.shape[1] // LANES):
        zc = z[:, c * LANES:(c + 1) * LANES]
        outs.append(zc * cos + pltpu.roll(zc, LANES - ROPE_HALF, 1) * sa + pltpu.roll(zc, ROPE_HALF, 1) * sb)
    return outs[0] if len(outs) == 1 else jnp.concatenate(outs, axis=1)


def _proj_kernel(x_ref, w_ref, cos_ref, sa_ref, sb_ref,
                 u_ref, q_ref, k_ref, v_ref, qi_ref, ki_ref, wi_ref, gc_ref, ga_ref):
    x = x_ref[...].astype(bf16)
    cos, sa, sb = cos_ref[:, :LANES], sa_ref[:, :LANES], sb_ref[:, :LANES]
    cos2, sa2, sb2 = cos_ref[:, LANES:], sa_ref[:, LANES:], sb_ref[:, LANES:]

    def mm(c0, c1):
        return jnp.dot(x, w_ref[:, c0:c1], preferred_element_type=f32)

    z = mm(0, 1024)
    u_ref[...] = z[:, :512] * jax.nn.sigmoid(z[:, 512:])
    z = mm(1024, 2048)
    q_ref[...] = (_rope(z, cos, sa, sb) * (HEAD_DIM ** -0.5)).astype(bf16)
    z = mm(2048, 2560)
    k_ref[...] = _rope(z[:, :256], cos, sa, sb)
    v_ref[...] = z[:, 256:]
    z = mm(2560, 3072)
    qi_ref[...] = _rope(z, cos, sa, sb).astype(bf16)
    z = mm(3072, 3200)
    z = _rope(z, cos2, sa2, sb2)
    ki_ref[...] = z[:, :IDX_DIM]
    wi_ref[...] = z[:, IDX_DIM:IDX_DIM + IDX_HEADS]
    gc_ref[...] = jax.nn.sigmoid(mm(3200, 4224))
    ga_ref[...] = jax.nn.sigmoid(mm(4224, 5248))


def _project(x2, w_pad, tabs, tm):
    R, D = x2.shape
    rt = tabs[0].shape[0]
    nt = rt // tm
    row = lambda i: (i, 0)
    tab = lambda i: (i % nt, 0)
    widths = [(512, f32), (1024, bf16), (256, f32), (256, f32), (512, bf16), (IDX_DIM, f32), (IDX_HEADS, f32),
              (1024, f32), (1024, f32)]
    return pl.pallas_call(
        _proj_kernel,
        grid=(R // tm,),
        in_specs=[pl.BlockSpec((tm, D), row),
                  pl.BlockSpec(w_pad.shape, lambda i: (0, 0)),
                  pl.BlockSpec((tm, 256), tab), pl.BlockSpec((tm, 256), tab), pl.BlockSpec((tm, 256), tab)],
        out_specs=[pl.BlockSpec((tm, w), row) for w, _ in widths],
        out_shape=[jax.ShapeDtypeStruct((R, w), dt) for w, dt in widths],
        compiler_params=_cparams(("parallel",)),
        name="proj",
    )(x2, w_pad, *tabs)


def _tile_lanes(t, width):
    return t if width == LANES else jnp.concatenate([t] * (width // LANES), axis=1)


def _select_tau(S, n_ch, k, n_adm):
    _, RW, CH = S.shape
    kf = float(k)
    inf = float("inf")
    rep = lambda a: jnp.broadcast_to(a, (RW, LANES))
    ones = jnp.ones((LANES, LANES), bf16)

    def mm_body(c, carry):
        mn, mx = carry
        s = S[c]
        mx = jnp.maximum(mx, jnp.max(s, axis=1, keepdims=True))
        mn = jnp.minimum(mn, jnp.min(jnp.where(s == -inf, inf, s), axis=1, keepdims=True))
        return mn, mx

    mn, mx = lax.fori_loop(0, n_ch, mm_body, (jnp.full((RW, 1), inf, f32), jnp.full((RW, 1), -inf, f32)))
    hi0 = mx + jnp.maximum(jnp.abs(mx), 1e-30) * (2.0 ** -20)

    def lane_sum(acc):
        return jnp.dot(acc.astype(bf16), ones, preferred_element_type=f32)

    def count_gt(tb):
        def body(c, acc):
            s = S[c]
            for l in range(CH // LANES):
                acc = acc + jnp.where(s[:, l * LANES:(l + 1) * LANES] > tb, 1.0, 0.0)
            return acc

        return lane_sum(lax.fori_loop(0, n_ch, body, jnp.zeros((RW, LANES), f32)))

    def probe(tb):
        RG = min(RW, 64)
        accs, bms = [], []
        for r0 in range(0, RW, RG):
            tr = tb[r0:r0 + RG]

            def body(c, carry, r0=r0, tr=tr):
                acc, bm = carry
                for l in range(CH // LANES):
                    sl = S[c, r0:r0 + RG, l * LANES:(l + 1) * LANES]
                    hit = sl >= tr
                    acc = acc + jnp.where(hit, 1.0, 0.0)
                    bm = jnp.maximum(bm, jnp.where(hit, -inf, sl))
                return acc, bm

            acc, bm = lax.fori_loop(0, n_ch, body,
                                    (jnp.zeros((RG, LANES), f32), jnp.full((RG, LANES), -inf, f32)))
            accs.append(acc)
            bms.append(bm)
        acc = jnp.concatenate(accs, axis=0)
        bm = jnp.concatenate(bms, axis=0)
        return lane_sum(acc), rep(jnp.max(bm, axis=1, keepdims=True))

    def to_key(x):
        b = lax.bitcast_convert_type(x, jnp.int32)
        return b ^ ((b >> 31) & 0x7FFFFFFF)

    n_adm_b = rep(n_adm)
    done0 = jnp.where(n_adm_b <= kf, 1.0, 0.0)
    tau0 = jnp.full((RW, LANES), -3e38, f32)
    zero = jnp.zeros((RW, LANES), f32)

    def cond(st):
        return jnp.logical_and(jnp.min(st[6]) < 0.5, st[9] < 400)

    def body(st):
        lo, hi, hb, cl, ch, tau, done, tie, stalled, it = st
        vhalf = lo + (hi - lo) * 0.5
        lk, hk = to_key(lo), to_key(hi)
        midk = (lk >> 1) + (hk >> 1) + (lk & hk & 1)
        khalf = lax.bitcast_convert_type(midk ^ ((midk >> 31) & 0x7FFFFFFF), f32)
        late = it >= 24
        half = jnp.where(late, khalf, vhalf)
        adj = jnp.where(late, jnp.where(midk == lk, 1.0, 0.0),
                        jnp.where((vhalf <= lo) | (vhalf >= hi), 1.0, 0.0))
        desc = ((cl - ch) <= 4.0) | (adj > 0.5) | (stalled > 0.5)
        mid = jnp.where(desc, hb, half)
        cnt, b = probe(mid)
        ge = cnt >= kf
        fin = (cnt == kf) | (ge & desc)
        keep = (done > 0.5) | (~fin)
        return (jnp.where(ge, mid, lo), jnp.where(ge, hi, mid), jnp.where(ge, hb, b),
                jnp.where(ge, cnt, cl), jnp.where(ge, ch, cnt),
                jnp.where(keep, tau, mid), jnp.where(keep, done, 1.0),
                jnp.where(keep, tie, jnp.where(cnt > kf, 1.0, 0.0)),
                jnp.where((cnt == cl) | (cnt == ch), 1.0, 0.0), it + 1)

    st = lax.while_loop(cond, body, (rep(mn), rep(hi0), rep(mx), n_adm_b, zero, tau0, done0, zero, zero,
                                     jnp.int32(0)))
    tau, tie = st[5], st[7]

    @pl.when(jnp.max(tie) > 0.5)
    def _():
        tau1, tie1 = tau[:, 0:1], tie[:, 0:1]
        need = kf - count_gt(tau)[:, 0:1]
        r = lax.broadcasted_iota(jnp.int32, (CH, CH), 0)
        c_ = lax.broadcasted_iota(jnp.int32, (CH, CH), 1)
        tri = jnp.where(r <= c_, 1.0, 0.0).astype(bf16)

        def fix(c, run):
            s = S[c]
            eq = (s == tau1) & (tie1 > 0.5)
            eqf = jnp.where(eq, 1.0, 0.0)
            pref = jnp.dot(eqf.astype(bf16), tri, preferred_element_type=f32) + run
            S[c] = jnp.where(eq & (pref > need), -inf, s)
            return run + jnp.sum(eqf, axis=1, keepdims=True)

        lax.fori_loop(0, n_ch, fix, jnp.zeros((RW, 1), f32))

    return tau


def _attn_prompt_kernel(qi_ref, wi_ref, kit_ref, q_ref, kt_ref, v_ref, o_ref, S, *, topk):
    qb = pl.program_id(1)
    QB = q_ref.shape[1]
    CH = S.shape[2]
    n_ch = (qb * QB + QB + CH - 1) // CH
    qpos = qb * QB + lax.broadcasted_iota(jnp.int32, (QB, 1), 0)
    wsc = wi_ref[0] * ((IDX_HEADS * IDX_DIM) ** -0.5)
    qi = jnp.concatenate([qi_ref[0, :, h * IDX_DIM:(h + 1) * IDX_DIM] for h in range(IDX_HEADS)], axis=0)

    def score_chunk(c, carry):
        s = jnp.dot(qi, kit_ref[0, c], preferred_element_type=f32)
        s = jnp.maximum(s, 0.0).reshape(IDX_HEADS, QB, CH)
        sc = s[0] * wsc[:, 0:1]
        for h in range(1, IDX_HEADS):
            sc = sc + s[h] * wsc[:, h:h + 1]
        kpos = c * CH + lax.broadcasted_iota(jnp.int32, (1, CH), 1)
        S[c] = jnp.where(kpos <= qpos, sc, -float("inf"))
        return carry

    lax.fori_loop(0, n_ch, score_chunk, 0)
    tau = _select_tau(S, n_ch, topk, (qpos + 1).astype(f32))

    G = N_HEADS // N_KV_HEADS
    taub = _tile_lanes(tau, CH)
    qgs = [jnp.concatenate([q_ref[0, :, (g * G + j) * HEAD_DIM:(g * G + j + 1) * HEAD_DIM] for j in range(G)],
                           axis=0) for g in range(N_KV_HEADS)]

    def chunk(c, carry):
        bias = jnp.where(S[c] >= taub, 0.0, NEG)[None]
        off = pl.multiple_of(c * CH, CH)
        out = []
        for g in range(N_KV_HEADS):
            m, acc = carry[g]
            s = jnp.dot(qgs[g], kt_ref[0, c, g], preferred_element_type=f32)
            s = (s.reshape(G, QB, CH) + bias).reshape(G * QB, CH)
            m_new = jnp.maximum(m, jnp.max(s, axis=1, keepdims=True))
            p = jnp.exp(s - m_new).astype(bf16)
            acc = jnp.exp(m - m_new) * acc + jnp.dot(p, v_ref[0, g, pl.ds(off, CH), :],
                                                      preferred_element_type=f32)
            out.append((m_new, acc))
        return tuple(out)

    init = tuple((jnp.full((G * QB, 1), NEG, f32), jnp.zeros((G * QB, LANES), f32)) for _ in range(N_KV_HEADS))
    res = lax.fori_loop(0, n_ch, chunk, init)
    for g in range(N_KV_HEADS):
        acc = res[g][1]
        o = acc[:, :HEAD_DIM] / acc[:, HEAD_DIM:HEAD_DIM + 1]
        for j in range(G):
            h = g * G + j
            o_ref[0, :, h * HEAD_DIM:(h + 1) * HEAD_DIM] = o[j * QB:(j + 1) * QB]


def _attn_prompt(q, k, v, qi, ki, wi):
    N, T, D = q.shape
    QB = min(Q_ROWS, T)
    topk = min(TOPK_MAX, T // 4)
    CH = min(KEY_CHUNK, T)
    NC = T // CH
    kit = ki.astype(bf16).reshape(N, NC, CH, IDX_DIM).transpose(0, 1, 3, 2)
    kt = k.astype(bf16).reshape(N, NC, CH, N_KV_HEADS, HEAD_DIM).transpose(0, 1, 3, 4, 2)
    vh = v.astype(bf16).reshape(N, T, N_KV_HEADS, HEAD_DIM).transpose(0, 2, 1, 3)
    vh = jnp.concatenate([vh, jnp.ones(vh.shape[:3] + (1,), bf16),
                          jnp.zeros(vh.shape[:3] + (LANES - HEAD_DIM - 1,), bf16)], axis=3)
    return pl.pallas_call(
        functools.partial(_attn_prompt_kernel, topk=topk),
        grid=(N, T // QB),
        in_specs=[pl.BlockSpec((1, QB, IDX_HEADS * IDX_DIM), lambda n, b: (n, b, 0)),
                  pl.BlockSpec((1, QB, IDX_HEADS), lambda n, b: (n, b, 0)),
                  pl.BlockSpec((1, NC, IDX_DIM, CH), lambda n, b: (n, 0, 0, 0)),
                  pl.BlockSpec((1, QB, D), lambda n, b: (n, b, 0)),
                  pl.BlockSpec((1, NC, N_KV_HEADS, HEAD_DIM, CH), lambda n, b: (n, 0, 0, 0, 0)),
                  pl.BlockSpec((1, N_KV_HEADS, T, LANES), lambda n, b: (n, 0, 0, 0))],
        out_specs=pl.BlockSpec((1, QB, D), lambda n, b: (n, b, 0)),
        out_shape=jax.ShapeDtypeStruct((N, T, D), f32),
        scratch_shapes=[pltpu.VMEM((NC, QB, CH), f32)],
        compiler_params=_cparams(("parallel", "arbitrary")),
        name="attn_prompt",
    )(qi, wi, kit, q, kt, vh)


def _sample_scores_kernel(pt_ref, qi_ref, w_ref, kin_ref, *rest, pp):
    pages = rest[:pp]
    sp_ref, sn_ref = rest[pp:]
    TQ = sp_ref.shape[2]
    qi = qi_ref[0]
    w = w_ref[0] * ((IDX_HEADS * IDX_DIM) ** -0.5)
    nt = (((1,), (1,)), ((), ()))

    def scores(keys):
        s = lax.dot_general(qi, keys, nt, preferred_element_type=f32)
        s = jnp.maximum(s, 0.0) * w
        return jnp.sum(s.reshape(TQ, IDX_HEADS, s.shape[1]), axis=1)

    keys = jnp.concatenate([p[0].astype(bf16) for p in pages], axis=0)
    sp_ref[0, 0] = scores(keys)

    @pl.when(pl.program_id(1) == 0)
    def _():
        s = scores(kin_ref[0])
        t = lax.broadcasted_iota(jnp.int32, s.shape, 0)
        j = lax.broadcasted_iota(jnp.int32, s.shape, 1)
        sn_ref[0] = jnp.where(j <= t, s, -float("inf"))


def _sample_select_kernel(sp_ref, sn_ref, bias_ref, S, *, topk, past, tq):
    NCP = sp_ref.shape[0]
    RW, CH = S.shape[1], S.shape[2]
    for c in range(NCP):
        S[c] = sp_ref[c]
    S[NCP] = jnp.concatenate([sn_ref[...], jnp.full((RW, CH - LANES), -float("inf"), f32)], axis=1)
    t = lax.broadcasted_iota(jnp.int32, (RW, 1), 0) % tq
    taub = _tile_lanes(_select_tau(S, NCP + 1, topk, (past + 1 + t).astype(f32)), CH)
    for c in range(NCP + 1):
        bias_ref[c] = jnp.where(S[c] >= taub, 0.0, NEG)


def _sample_attend_kernel(pt_ref, q_ref, bp_ref, bn_ref, kn_ref, vn_ref, *rest, pp):
    kp = rest[:pp]
    vp = rest[pp:2 * pp]
    o_ref, m_ref, l_ref, acc_ref = rest[2 * pp:]
    p_id = pl.program_id(1)
    q = q_ref[0]
    R = q.shape[0]
    TQ = bp_ref.shape[2]
    nt = (((1,), (1,)), ((), ()))

    @pl.when(p_id == 0)
    def _():
        m_ref[...] = jnp.full(m_ref.shape, NEG, f32)
        l_ref[...] = jnp.zeros(l_ref.shape, f32)
        acc_ref[...] = jnp.zeros(acc_ref.shape, f32)

    def step(keys, vals, bias):
        s = lax.dot_general(q, keys, nt, preferred_element_type=f32)
        L = s.shape[1]
        b = jnp.broadcast_to(bias[:, None, :], (TQ, R // TQ, L)).reshape(R, L)
        s = jnp.where(b < 0.0, NEG, s)
        m = m_ref[...]
        m_new = jnp.maximum(m, jnp.max(s, axis=1, keepdims=True))
        p = jnp.exp(s - m_new)
        a = jnp.exp(m - m_new)
        l_ref[...] = a * l_ref[...] + jnp.sum(p, axis=1, keepdims=True)
        acc_ref[...] = a * acc_ref[...] + jnp.dot(p.astype(bf16), vals, preferred_element_type=f32)
        m_ref[...] = m_new

    keys = jnp.concatenate([p[0].astype(bf16) for p in kp], axis=0)
    vals = jnp.concatenate([p[0].astype(bf16) for p in vp], axis=0)
    step(keys, vals, bp_ref[0, 0])

    @pl.when(p_id == pl.num_programs(1) - 1)
    def _():
        step(kn_ref[0], vn_ref[0], bn_ref[0])
        o_ref[0] = acc_ref[...] / l_ref[...]


def _attn_sample(q, k_new, v_new, qi, ki_new, wi, cache_k, cache_v, cache_kidx, page_table):
    N, tq, _ = q.shape
    n_pages = page_table.shape[1]
    past = n_pages * PAGE
    topk = min(TOPK_MAX, (past + tq) // 4)
    PP = math.gcd(PAGES_PER_STEP, n_pages)
    NP = n_pages // PP
    LP = PP * PAGE
    n_pool = cache_k.shape[0]
    KV = N_KV_HEADS * HEAD_DIM

    qi2 = qi.reshape(N, tq * IDX_HEADS, IDX_DIM)
    w2 = wi.reshape(N, tq * IDX_HEADS, 1)
    pad_rows = lambda a: jnp.pad(a, ((0, 0), (0, LANES - tq), (0, 0)))
    kin = pad_rows(ki_new.astype(bf16))

    def page_spec(shape, j):
        return pl.BlockSpec(shape, lambda n, p, pt: (pt[n, p * PP + j], 0, 0))

    sp, sn = pl.pallas_call(
        functools.partial(_sample_scores_kernel, pp=PP),
        grid_spec=pltpu.PrefetchScalarGridSpec(
            num_scalar_prefetch=1, grid=(N, NP),
            in_specs=[pl.BlockSpec((1, tq * IDX_HEADS, IDX_DIM), lambda n, p, pt: (n, 0, 0)),
                      pl.BlockSpec((1, tq * IDX_HEADS, 1), lambda n, p, pt: (n, 0, 0)),
                      pl.BlockSpec((1, LANES, IDX_DIM), lambda n, p, pt: (n, 0, 0))]
                     + [page_spec((1, PAGE, IDX_DIM), j) for j in range(PP)],
            out_specs=[pl.BlockSpec((1, 1, tq, LP), lambda n, p, pt: (n, p, 0, 0)),
                       pl.BlockSpec((1, tq, LANES), lambda n, p, pt: (n, 0, 0))]),
        out_shape=[jax.ShapeDtypeStruct((N, NP, tq, LP), f32), jax.ShapeDtypeStruct((N, tq, LANES), f32)],
        compiler_params=_cparams(("parallel", "arbitrary")),
        name="sample_scores",
    )(page_table, qi2, w2, kin, *([cache_kidx] * PP))

    CH = KEY_CHUNK
    NCP = past // CH
    RW = N * tq
    sp2 = sp.transpose(0, 2, 1, 3).reshape(RW, NCP, CH).transpose(1, 0, 2)
    bias = pl.pallas_call(
        functools.partial(_sample_select_kernel, topk=topk, past=past, tq=tq),
        out_shape=jax.ShapeDtypeStruct((NCP + 1, RW, CH), f32),
        scratch_shapes=[pltpu.VMEM((NCP + 1, RW, CH), f32)],
        compiler_params=pltpu.CompilerParams(vmem_limit_bytes=VMEM_LIMIT),
        name="sample_select",
    )(sp2, sn.reshape(RW, LANES))
    bp = bias[:NCP].transpose(1, 0, 2).reshape(N, tq, NP, LP).transpose(0, 2, 1, 3)
    bn = bias[NCP, :, :LANES].reshape(N, tq, LANES)

    G = N_HEADS // N_KV_HEADS
    q5 = q.reshape(N, tq, N_KV_HEADS, G, 1, HEAD_DIM)
    eye = jnp.eye(N_KV_HEADS, dtype=q.dtype).reshape(1, 1, N_KV_HEADS, 1, N_KV_HEADS, 1)
    qbd = (q5 * eye).reshape(N, tq * N_HEADS, KV)
    kn = pad_rows(k_new.astype(bf16))
    vn = pad_rows(v_new.astype(bf16))
    ck = cache_k.reshape(n_pool, PAGE, KV)
    cv = cache_v.reshape(n_pool, PAGE, KV)
    R = tq * N_HEADS
    o = pl.pallas_call(
        functools.partial(_sample_attend_kernel, pp=PP),
        grid_spec=pltpu.PrefetchScalarGridSpec(
            num_scalar_prefetch=1, grid=(N, NP),
            in_specs=[pl.BlockSpec((1, R, KV), lambda n, p, pt: (n, 0, 0)),
                      pl.BlockSpec((1, 1, tq, LP), lambda n, p, pt: (n, p, 0, 0)),
                      pl.BlockSpec((1, tq, LANES), lambda n, p, pt: (n, 0, 0)),
                      pl.BlockSpec((1, LANES, KV), lambda n, p, pt: (n, 0, 0)),
                      pl.BlockSpec((1, LANES, KV), lambda n, p, pt: (n, 0, 0))]
                     + [page_spec((1, PAGE, KV), j) for j in range(PP)]
                     + [page_spec((1, PAGE, KV), j) for j in range(PP)],
            out_specs=pl.BlockSpec((1, R, KV), lambda n, p, pt: (n, 0, 0)),
            scratch_shapes=[pltpu.VMEM((R, 1), f32), pltpu.VMEM((R, 1), f32), pltpu.VMEM((R, KV), f32)]),
        out_shape=jax.ShapeDtypeStruct((N, R, KV), f32),
        compiler_params=_cparams(("parallel", "arbitrary")),
        name="sample_attend",
    )(page_table, qbd, bp, bn, kn, vn, *([ck] * PP), *([cv] * PP))
    o6 = o.reshape(N, tq, N_KV_HEADS, G, N_KV_HEADS, HEAD_DIM)
    sel = jnp.eye(N_KV_HEADS, dtype=f32).reshape(1, 1, N_KV_HEADS, 1, N_KV_HEADS, 1)
    return jnp.sum(o6 * sel, axis=4).reshape(N, tq, N_HEADS * HEAD_DIM)


HALO = 32


def _conv_kernel(u_ref, prev_ref, cw_ref, cb_ref, g_ref, b_ref, wo_ref, o_ref, buf):
    tt = u_ref.shape[1]

    @pl.when(pl.program_id(1) == 0)
    def _():
        buf[0:HALO] = prev_ref[0]

    @pl.when(pl.program_id(1) > 0)
    def _():
        buf[0:HALO] = buf[tt:tt + HALO]

    buf[HALO:HALO + tt] = u_ref[0]
    off = HALO - (D_CONV_W - 1)
    y = buf[off:off + tt] * cw_ref[0:1, :]
    for j in range(1, D_CONV_W):
        y = y + buf[off + j:off + j + tt] * cw_ref[j:j + 1, :]
    y = y + cb_ref[...]
    mu = jnp.mean(y, axis=-1, keepdims=True)
    yc = y - mu
    var = jnp.mean(yc * yc, axis=-1, keepdims=True)
    y = yc * lax.rsqrt(var + LN_EPS) * g_ref[...] + b_ref[...]
    y = y * jax.nn.sigmoid(y)
    o_ref[0] = jnp.dot(y.astype(bf16), wo_ref[...], preferred_element_type=f32)


def _conv_module(u, prev, conv_w, conv_b, ln_g, ln_b, w_conv_out_bf, tt):
    N, T, C = u.shape
    D = w_conv_out_bf.shape[1]
    prev_pad = jnp.pad(prev, ((0, 0), (HALO - prev.shape[1], 0), (0, 0)))
    cw = jnp.pad(conv_w, ((0, 32 - conv_w.shape[0]), (0, 0)))
    row2 = lambda a: a.reshape(1, -1)
    return pl.pallas_call(
        _conv_kernel,
        grid=(N, T // tt),
        in_specs=[pl.BlockSpec((1, tt, C), lambda n, t: (n, t, 0)),
                  pl.BlockSpec((1, HALO, C), lambda n, t: (n, 0, 0)),
                  pl.BlockSpec((32, C), lambda n, t: (0, 0)),
                  pl.BlockSpec((1, C), lambda n, t: (0, 0)),
                  pl.BlockSpec((1, C), lambda n, t: (0, 0)),
                  pl.BlockSpec((1, C), lambda n, t: (0, 0)),
                  pl.BlockSpec((C, D), lambda n, t: (0, 0))],
        out_specs=pl.BlockSpec((1, tt, D), lambda n, t: (n, t, 0)),
        out_shape=jax.ShapeDtypeStruct((N, T, D), f32),
        scratch_shapes=[pltpu.VMEM((HALO + tt + 8, C), f32)],
        compiler_params=_cparams(("parallel", "arbitrary")),
        name="conv_module",
    )(u, prev_pad, cw, row2(conv_b), row2(ln_g), row2(ln_b), w_conv_out_bf)


def _merge_kernel(x_ref, co_ref, ao_ref, gc_ref, ga_ref, wo_ref, g_ref, b_ref, ht_ref, htb_ref):
    m = gc_ref[...] * co_ref[...] + ga_ref[...] * ao_ref[...]
    z = ALPHA * x_ref[...] + jnp.dot(m.astype(bf16), wo_ref[...], preferred_element_type=f32)
    mu = jnp.mean(z, axis=-1, keepdims=True)
    zc = z - mu
    var = jnp.mean(zc * zc, axis=-1, keepdims=True)
    ht = (zc * lax.rsqrt(var + LN_EPS) * g_ref[...] + b_ref[...]).T
    ht_ref[...] = ht
    htb_ref[...] = ht.astype(bf16)


def _merge(x2, conv_o, attn_o, sgc, sga, w_out_bf, ln_g, ln_b, tm):
    R, D = x2.shape
    row = lambda i: (i, 0)
    cst = lambda i: (0, 0)
    col = lambda i: (0, i)
    return pl.pallas_call(
        _merge_kernel,
        grid=(R // tm,),
        in_specs=[pl.BlockSpec((tm, D), row)] * 5
                 + [pl.BlockSpec((D, D), cst), pl.BlockSpec((1, D), cst), pl.BlockSpec((1, D), cst)],
        out_specs=[pl.BlockSpec((D, tm), col), pl.BlockSpec((D, tm), col)],
        out_shape=[jax.ShapeDtypeStruct((D, R), f32), jax.ShapeDtypeStruct((D, R), bf16)],
        compiler_params=_cparams(("parallel",)),
        name="merge",
    )(x2, conv_o, attn_o, sgc, sga, w_out_bf, ln_g.reshape(1, D), ln_b.reshape(1, D))


def _extract16(s, idx, exact):
    rank = jnp.full(s.shape, float(PEER_TOPK), f32)
    tops = []
    for r in range(PEER_TOPK):
        m = jnp.max(s, axis=0, keepdims=True)
        if exact:
            pick = idx == jnp.min(jnp.where(s == m, idx, 1e9), axis=0, keepdims=True)
        else:
            pick = s == m
        rank = jnp.where(pick, float(r), rank)
        s = jnp.where(pick, -float("inf"), s)
        tops.append(m)
    return rank, jnp.concatenate(tops, axis=0)


def _route_head(s1, s2, exact):
    K, Tn = s1.shape
    T16 = float(PEER_TOPK)
    ninf = -float("inf")
    rows = lax.broadcasted_iota(jnp.int32, (K, Tn), 0).astype(f32)
    sub = lax.broadcasted_iota(jnp.int32, (8, Tn), 0)
    subf = sub.astype(f32)
    r1, t1 = _extract16(s1, rows, exact)
    r2, t2 = _extract16(s2, rows, exact)
    limits = [16, 8, 5, 4, 3, 2, 2, 2]
    cands, idxs = [], []
    cands.append(t1[0:1] + t2[0:8]); idxs.append(subf)
    cands.append(t1[0:1] + t2[8:16]); idxs.append(subf + 8.0)
    for a in range(1, 8):
        cands.append(jnp.where(sub < limits[a], t1[a:a + 1] + t2[0:8], ninf))
        idxs.append(subf + float(16 * a))
    cands.append(t1[8:16] + t2[0:1]); idxs.append((subf + 8.0) * 16.0)
    cand = jnp.concatenate(cands, axis=0)
    cidx = jnp.concatenate(idxs, axis=0)
    ecand = jnp.exp(cand - cand[0:1])
    rc, _ = _extract16(cand, cidx, exact)
    picked = jnp.where(rc < T16, 1.0, 0.0)
    z = jnp.sum(picked * ecand, axis=0, keepdims=True)
    la = [jnp.sum(picked[0:16], axis=0, keepdims=True)]
    for a in range(1, 8):
        la.append(jnp.sum(picked[8 + 8 * a:16 + 8 * a], axis=0, keepdims=True))
    ltail = picked[72:80]
    c = jnp.zeros(s1.shape, f32)
    for a in range(8):
        c = jnp.where(r1 == float(a), la[a], c)
    for a in range(8, 16):
        c = jnp.where(r1 == float(a), ltail[a - 8:a - 7], c)
    in1 = r1 < T16
    in2 = r2 < T16
    e1 = jnp.where(in1, jnp.exp(s1 - t1[0:1]), 0.0) / z
    e2 = jnp.where(in2, jnp.exp(s2 - t2[0:1]), 0.0)
    n1 = jnp.sum(jnp.where(in1, 1.0, 0.0), axis=0, keepdims=True)
    n2 = jnp.sum(jnp.where(in2, 1.0, 0.0), axis=0, keepdims=True)
    nc = jnp.sum(picked, axis=0, keepdims=True)
    ok = (n1 == T16) & (n2 == T16) & (nc == T16)
    return r2, e2, c, e1, ok


def _route_kernel(ht_ref, wq_ref, k1_ref, k2_ref, r2_ref, e2_ref, c_ref, e1_ref):
    qh = jnp.dot(wq_ref[...], ht_ref[...], preferred_element_type=f32)
    half = PEER_DQ // 2

    def store(h, r2, e2, c, e1):
        r2_ref[h] = r2.astype(bf16)
        e2_ref[h] = e2.astype(bf16)
        c_ref[h] = c
        e1_ref[h] = e1

    for h in range(PEER_HEADS):
        q1 = qh[h * PEER_DQ:h * PEER_DQ + half].astype(bf16)
        q2 = qh[h * PEER_DQ + half:(h + 1) * PEER_DQ].astype(bf16)
        s1 = jnp.dot(k1_ref[h], q1, preferred_element_type=f32)
        s2 = jnp.dot(k2_ref[h], q2, preferred_element_type=f32)
        r2, e2, c, e1, ok = _route_head(s1, s2, exact=False)
        store(h, r2, e2, c, e1)

        @pl.when(jnp.min(jnp.where(ok, 1.0, 0.0)) < 0.5)
        def _(h=h, s1=s1, s2=s2):
            store(h, *_route_head(s1, s2, exact=True)[:4])


def _route(ht_bf, wqt_bf, k1_bf, k2_bf, tn):
    D, R = ht_bf.shape
    shp = lambda dt: jax.ShapeDtypeStruct((PEER_HEADS, PEER_NKEYS, R), dt)
    ospec = pl.BlockSpec((PEER_HEADS, PEER_NKEYS, tn), lambda i: (0, 0, i))
    return pl.pallas_call(
        _route_kernel,
        grid=(R // tn,),
        in_specs=[pl.BlockSpec((D, tn), lambda i: (0, i)),
                  pl.BlockSpec(wqt_bf.shape, lambda i: (0, 0)),
                  pl.BlockSpec(k1_bf.shape, lambda i: (0, 0, 0)),
                  pl.BlockSpec(k2_bf.shape, lambda i: (0, 0, 0))],
        out_specs=[ospec] * 4,
        out_shape=[shp(bf16), shp(bf16), shp(f32), shp(f32)],
        compiler_params=_cparams(("parallel",)),
        name="peer_route",
    )(ht_bf, wqt_bf, k1_bf, k2_bf)


ET = 1024
PEER_SUBTILE = 256


def _peer_kernel(htb_ref, ht_ref, u_ref, vt_ref, r2_ref, e2_ref, c_ref, e1_ref, g_ref, b_ref,
                 y_ref, acc_ref, p_ref):
    et = pl.program_id(1)
    NK = PEER_NKEYS

    @pl.when(et == 0)
    def _():
        acc_ref[...] = jnp.zeros(acc_ref.shape, f32)

    Tn = htb_ref.shape[1]
    SL = 16
    TS = p_ref.shape[2]
    zero = jnp.zeros((), bf16)
    acts = [[jnp.dot(u_ref[ii * NK:(ii + 1) * NK, :], htb_ref[:, ts * TS:(ts + 1) * TS],
                     preferred_element_type=f32) for ii in range(ET // NK)] for ts in range(Tn // TS)]
    for ts in range(Tn // TS):
        tok = slice(ts * TS, (ts + 1) * TS)
        for ii in range(ET // NK):
            a = acts[ts][ii]
            gl = (0.5 * a * (1.0 + lax.erf(a * (2.0 ** -0.5)))).astype(bf16)
            w = [jnp.zeros((SL, TS), bf16) for _ in range(NK // SL)]
            for h in range(PEER_HEADS):
                cb = jnp.broadcast_to(c_ref[h, ii:ii + 1, tok], (SL, TS)).astype(bf16)
                eb = jnp.broadcast_to(e1_ref[h, ii:ii + 1, tok], (SL, TS)).astype(bf16)
                for s in range(NK // SL):
                    rows = slice(s * SL, (s + 1) * SL)
                    w[s] = w[s] + jnp.where(r2_ref[h, rows, tok] < cb, eb * e2_ref[h, rows, tok], zero)
            for s in range(NK // SL):
                p_ref[ts, ii * NK + s * SL:ii * NK + (s + 1) * SL, :] = w[s] * gl[s * SL:(s + 1) * SL]
        acc_ref[:, tok] += jnp.dot(vt_ref[...], p_ref[ts], preferred_element_type=f32)

    @pl.when(et == pl.num_programs(1) - 1)
    def _():
        z = ALPHA * ht_ref[...] + acc_ref[...]
        mu = jnp.mean(z, axis=0, keepdims=True)
        zc = z - mu
        var = jnp.mean(zc * zc, axis=0, keepdims=True)
        y_ref[...] = (zc * lax.rsqrt(var + LN_EPS) * g_ref[...] + b_ref[...]).T


def _peer_dense(ht_bf, ht, u_bf, vt_bf, r2, e2, c, e1, ln_g, ln_b, tn):
    D, R = ht.shape
    E = u_bf.shape[0]
    NI = ET // PEER_NKEYS
    ts = min(tn, PEER_SUBTILE)
    tok = lambda i, e: (0, i)
    rt = pl.BlockSpec((PEER_HEADS, PEER_NKEYS, tn), lambda i, e: (0, 0, i))
    ri = pl.BlockSpec((PEER_HEADS, NI, tn), lambda i, e: (0, e, i))
    return pl.pallas_call(
        _peer_kernel,
        grid=(R // tn, E // ET),
        in_specs=[pl.BlockSpec((D, tn), tok), pl.BlockSpec((D, tn), tok),
                  pl.BlockSpec((ET, D), lambda i, e: (e, 0)),
                  pl.BlockSpec((D, ET), lambda i, e: (0, e)),
                  rt, rt, ri, ri,
                  pl.BlockSpec((D, 1), lambda i, e: (0, 0)), pl.BlockSpec((D, 1), lambda i, e: (0, 0))],
        out_specs=pl.BlockSpec((tn, D), lambda i, e: (i, 0)),
        out_shape=jax.ShapeDtypeStruct((R, D), f32),
        scratch_shapes=[pltpu.VMEM((D, tn), f32), pltpu.VMEM((tn // ts, ET, ts), bf16)],
        compiler_params=_cparams(("parallel", "arbitrary")),
        name="peer_dense",
    )(ht_bf, ht, u_bf, vt_bf, r2, e2, c, e1, ln_g.reshape(D, 1), ln_b.reshape(D, 1))


def _channel_mix(ht, ht_bf, wqt_bf, k1_bf, k2_bf, u_bf, vt_bf, ln_g, ln_b, tn_route, tn_dense):
    r2, e2, c, e1 = _route(ht_bf, wqt_bf, k1_bf, k2_bf, tn_route)
    return _peer_dense(ht_bf, ht, u_bf, vt_bf, r2, e2, c, e1, ln_g, ln_b, tn_dense)


def _prep_weights(w_in, w_conv_out, w_out, peer_w_query, peer_keys1, peer_keys2, peer_u, peer_v):
    D = w_in.shape[0]
    split = 2 * (D // 2) + N_HEADS * HEAD_DIM + 2 * N_KV_HEADS * HEAD_DIM + IDX_HEADS * IDX_DIM + IDX_DIM + IDX_HEADS
    pad = (-split) % LANES
    w_pad = jnp.concatenate([w_in[:, :split], jnp.zeros((D, pad), w_in.dtype), w_in[:, split:]], axis=1).astype(bf16)
    wqt = peer_w_query.reshape(D, PEER_HEADS * PEER_DQ).T.astype(bf16)
    return dict(w_pad=w_pad, wco=w_conv_out.astype(bf16), wo=w_out.astype(bf16), wqt=wqt,
                k1=peer_keys1.astype(bf16), k2=peer_keys2.astype(bf16),
                u=peer_u.astype(bf16), vt=peer_v.T.astype(bf16))


def _group(x, pos, prev, W, conv_w, conv_b, conv_ln_g, conv_ln_b, ln1_g, ln1_b, ln2_g, ln2_b,
           attn_fn, tm, tt, tn_route, tn_dense):
    N, T, D = x.shape
    R = N * T
    x2 = x.reshape(R, D)
    tabs = _rope_tables(pos if T % tm == 0 else jnp.tile(pos, tm // T))
    u, q, k, v, qi, ki, wi, sgc, sga = _project(x2, W["w_pad"], tabs, tm)
    r3 = lambda a: a.reshape(N, T, a.shape[-1])
    attn_o = attn_fn(r3(q), r3(k), r3(v), r3(qi), r3(ki), r3(wi))
    u3 = r3(u)
    if T % tt == 0:
        conv_o = _conv_module(u3, prev, conv_w, conv_b, conv_ln_g, conv_ln_b, W["wco"], tt)
    else:
        up = jnp.pad(u3, ((0, 0), (0, tt - T), (0, 0)))
        conv_o = _conv_module(up, prev, conv_w, conv_b, conv_ln_g, conv_ln_b, W["wco"], tt)[:, :T]
    ht, ht_bf = _merge(x2, conv_o.reshape(R, D), attn_o.reshape(R, D), sgc, sga, W["wo"], ln1_g, ln1_b, tm)
    y2 = _channel_mix(ht, ht_bf, W["wqt"], W["k1"], W["k2"], W["u"], W["vt"], ln2_g, ln2_b, tn_route, tn_dense)
    return y2.reshape(N, T, D), k, v, ki, u3


def kernel(x_prompt, x_sample, cache_k, cache_v, cache_kidx, state_conv, page_table, w_in, conv_w, conv_b,
           conv_ln_g, conv_ln_b, w_conv_out, w_out, ln1_g, ln1_b, peer_w_query, peer_keys1, peer_keys2,
           peer_u, peer_v, ln2_g, ln2_b):
    W = _prep_weights(w_in, w_conv_out, w_out, peer_w_query, peer_keys1, peer_keys2, peer_u, peer_v)
    common = (W, conv_w, conv_b, conv_ln_g, conv_ln_b, ln1_g, ln1_b, ln2_g, ln2_b)
    C = conv_w.shape[1]
    keep = D_CONV_W - 1

    N, T, D = x_prompt.shape
    tm = min(256, N * T)
    yp, k_p, v_p, kidx_p, u_p = _group(
        x_prompt, jnp.arange(T, dtype=jnp.int32), jnp.zeros((N, keep, C), f32), *common,
        _attn_prompt, tm, min(512, T), min(256, N * T), min(512, N * T))
    conv_p = u_p[:, T - keep:]

    NS, tq, _ = x_sample.shape
    past = page_table.shape[1] * PAGE
    attn_s = functools.partial(_attn_sample, cache_k=cache_k, cache_v=cache_v, cache_kidx=cache_kidx,
                               page_table=page_table)
    rs = NS * tq
    ys, k_s, v_s, kidx_s, u_s = _group(
        x_sample, past + jnp.arange(tq, dtype=jnp.int32), state_conv, *common,
        lambda q, k, v, qi, ki, wi: attn_s(q, k, v, qi, ki, wi), rs, 8, rs, rs)
    conv_s = jnp.concatenate([state_conv, u_s], axis=1)[:, -keep:]

    kv4 = lambda a, n, t: a.reshape(n, t, N_KV_HEADS, HEAD_DIM)
    return (yp, ys, kv4(k_p, N, T), kv4(v_p, N, T), kidx_p.reshape(N, T, IDX_DIM), conv_p,
            kv4(k_s, NS, tq), kv4(v_s, NS, tq), kidx_s.reshape(NS, tq, IDX_DIM), conv_s)
```

```python
import functools
import math

import jax
import jax.numpy as jnp
from jax import lax
from jax.experimental import pallas as pl
from jax.experimental.pallas import tpu as pltpu

f32 = jnp.float32
bf16 = jnp.bfloat16

D_CONV_W = 31
N_HEADS = 16
N_KV_HEADS = 4
HEAD_DIM = 64
IDX_HEADS = 8
IDX_DIM = 64
TOPK_MAX = 256
ROPE_THETA = 500000.0
ROPE_HALF = 8
PAGE = 128
PEER_HEADS = 8
PEER_NKEYS = 128
PEER_DQ = 128
PEER_TOPK = 16
ALPHA = 2.0 ** 0.25
LN_EPS = 1e-5

LANES = 128
VMEM_LIMIT = 56 * 1024 * 1024
NEG = -1e30
KEY_CHUNK = 512
Q_ROWS = 256
ATTN_ROWS = 256
PAGES_PER_STEP = 16


def _cparams(sem):
    return pltpu.CompilerParams(dimension_semantics=sem, vmem_limit_bytes=VMEM_LIMIT)


def _rope_tables(pos):
    inv = jnp.power(ROPE_THETA, -jnp.arange(ROPE_HALF, dtype=f32) / ROPE_HALF)
    ang = pos.astype(f32)[:, None] * inv
    cos, sin = jnp.cos(ang), jnp.sin(ang)
    d = jnp.arange(HEAD_DIM)
    cosp = jnp.where(d < 2 * ROPE_HALF, cos[:, d % ROPE_HALF], 1.0)
    sap = jnp.where(d < ROPE_HALF, -sin[:, d % ROPE_HALF], 0.0)
    sbp = jnp.where((d >= ROPE_HALF) & (d < 2 * ROPE_HALF), sin[:, d % ROPE_HALF], 0.0)
    one = jnp.ones_like(cosp)
    zero = jnp.zeros_like(cosp)
    return (jnp.concatenate([cosp, cosp, cosp, one], axis=1),
            jnp.concatenate([sap, sap, sap, zero], axis=1),
            jnp.concatenate([sbp, sbp, sbp, zero], axis=1))


def _rope(z, cos, sa, sb):
    outs = []
    for c in range(z.shape[1] // LANES):
        zc = z[:, c * LANES:(c + 1) * LANES]
        outs.append(zc * cos + pltpu.roll(zc, LANES - ROPE_HALF, 1) * sa + pltpu.roll(zc, ROPE_HALF, 1) * sb)
    return outs[0] if len(outs) == 1 else jnp.concatenate(outs, axis=1)


def _proj_kernel(x_ref, w_ref, cos_ref, sa_ref, sb_ref,
                 u_ref, q_ref, k_ref, v_ref, qi_ref, ki_ref, wi_ref, gc_ref, ga_ref):
    x = x_ref[...].astype(bf16)
    cos, sa, sb = cos_ref[:, :LANES], sa_ref[:, :LANES], sb_ref[:, :LANES]
    cos2, sa2, sb2 = cos_ref[:, LANES:], sa_ref[:, LANES:], sb_ref[:, LANES:]

    def mm(c0, c1):
        return jnp.dot(x, w_ref[:, c0:c1], preferred_element_type=f32)

    z = mm(0, 1024)
    u_ref[...] = z[:, :512] * jax.nn.sigmoid(z[:, 512:])
    z = mm(1024, 2048)
    q_ref[...] = (_rope(z, cos, sa, sb) * (HEAD_DIM ** -0.5)).astype(bf16)
    z = mm(2048, 2560)
    k_ref[...] = _rope(z[:, :256], cos, sa, sb)
    v_ref[...] = z[:, 256:]
    z = mm(2560, 3072)
    qi_ref[...] = _rope(z, cos, sa, sb).astype(bf16)
    z = mm(3072, 3200)
    z = _rope(z, cos2, sa2, sb2)
    ki_ref[...] = z[:, :IDX_DIM]
    wi_ref[...] = z[:, IDX_DIM:IDX_DIM + IDX_HEADS]
    gc_ref[...] = jax.nn.sigmoid(mm(3200, 4224))
    ga_ref[...] = jax.nn.sigmoid(mm(4224, 5248))


def _project(x2, w_pad, tabs, tm):
    R, D = x2.shape
    rt = tabs[0].shape[0]
    nt = rt // tm
    row = lambda i: (i, 0)
    tab = lambda i: (i % nt, 0)
    widths = [(512, f32), (1024, bf16), (256, f32), (256, f32), (512, bf16), (IDX_DIM, f32), (IDX_HEADS, f32),
              (1024, f32), (1024, f32)]
    return pl.pallas_call(
        _proj_kernel,
        grid=(R // tm,),
        in_specs=[pl.BlockSpec((tm, D), row),
                  pl.BlockSpec(w_pad.shape, lambda i: (0, 0)),
                  pl.BlockSpec((tm, 256), tab), pl.BlockSpec((tm, 256), tab), pl.BlockSpec((tm, 256), tab)],
        out_specs=[pl.BlockSpec((tm, w), row) for w, _ in widths],
        out_shape=[jax.ShapeDtypeStruct((R, w), dt) for w, dt in widths],
        compiler_params=_cparams(("parallel",)),
        name="proj",
    )(x2, w_pad, *tabs)


def _tile_lanes(t, width):
    return t if width == LANES else jnp.concatenate([t] * (width // LANES), axis=1)


def _select_tau(S, n_ch, k, n_adm):
    _, RW, CH = S.shape
    kf = float(k)
    inf = float("inf")
    rep = lambda a: jnp.broadcast_to(a, (RW, LANES))
    ones = jnp.ones((LANES, LANES), bf16)

    def mm_body(c, carry):
        mn, mx = carry
        s = S[c]
        mx = jnp.maximum(mx, jnp.max(s, axis=1, keepdims=True))
        mn = jnp.minimum(mn, jnp.min(jnp.where(s == -inf, inf, s), axis=1, keepdims=True))
        return mn, mx

    mn, mx = lax.fori_loop(0, n_ch, mm_body, (jnp.full((RW, 1), inf, f32), jnp.full((RW, 1), -inf, f32)))
    hi0 = mx + jnp.maximum(jnp.abs(mx), 1e-30) * (2.0 ** -20)

    def lane_sum(acc):
        return jnp.dot(acc.astype(bf16), ones, preferred_element_type=f32)

    def count_gt(tb):
        def body(c, acc):
            s = S[c]
            for l in range(CH // LANES):
                acc = acc + jnp.where(s[:, l * LANES:(l + 1) * LANES] > tb, 1.0, 0.0)
            return acc

        return lane_sum(lax.fori_loop(0, n_ch, body, jnp.zeros((RW, LANES), f32)))

    def probe(tb):
        RG = min(RW, 64)
        accs, bms = [], []
        for r0 in range(0, RW, RG):
            tr = tb[r0:r0 + RG]

            def body(c, carry, r0=r0, tr=tr):
                acc, bm = carry
                for l in range(CH // LANES):
                    sl = S[c, r0:r0 + RG, l * LANES:(l + 1) * LANES]
                    hit = sl >= tr
                    acc = acc + jnp.where(hit, 1.0, 0.0)
                    bm = jnp.maximum(bm, jnp.where(hit, -inf, sl))
                return acc, bm

            acc, bm = lax.fori_loop(0, n_ch, body,
                                    (jnp.zeros((RG, LANES), f32), jnp.full((RG, LANES), -inf, f32)))
            accs.append(acc)
            bms.append(bm)
        acc = jnp.concatenate(accs, axis=0)
        bm = jnp.concatenate(bms, axis=0)
        return lane_sum(acc), rep(jnp.max(bm, axis=1, keepdims=True))

    def to_key(x):
        b = lax.bitcast_convert_type(x, jnp.int32)
        return b ^ ((b >> 31) & 0x7FFFFFFF)

    n_adm_b = rep(n_adm)
    all_sel = n_adm_b <= kf
    RUN, STALL, DONE, TIED = 0.0, 1.0, 2.0, 3.0

    def cond(st):
        return jnp.logical_and(jnp.min(st[5]) < DONE, st[6] < 400)

    def body(st):
        lo, hi, hb, cl, ch, flag, it = st

        def value_half():
            h = lo + (hi - lo) * 0.5
            return h, jnp.where(h <= lo, 1.0, jnp.where(h >= hi, 1.0, 0.0))

        def key_half():
            lk, hk = to_key(lo), to_key(hi)
            midk = (lk >> 1) + (hk >> 1) + (lk & hk & 1)
            h = lax.bitcast_convert_type(midk ^ ((midk >> 31) & 0x7FFFFFFF), f32)
            return h, jnp.where(midk == lk, 1.0, 0.0)

        half, adj = lax.cond(it >= 24, key_half, value_half)
        descf = jnp.where(cl - ch <= 4.0, 1.0, jnp.where(flag == STALL, 1.0, adj))
        desc = descf > 0.5
        mid = jnp.where(desc, hb, half)
        cnt, b = probe(mid)
        ge = cnt >= kf
        fin = jnp.where(cnt == kf, 1.0, jnp.where(ge, descf, 0.0))
        idle = jnp.where(cnt == cl, STALL, jnp.where(cnt == ch, STALL, RUN))
        nxt = jnp.where(fin > 0.5, jnp.where(cnt > kf, TIED, DONE), idle)
        running = flag < DONE
        return (jnp.where(running, jnp.where(ge, mid, lo), lo), jnp.where(ge, hi, mid), jnp.where(ge, hb, b),
                jnp.where(ge, cnt, cl), jnp.where(ge, ch, cnt), jnp.where(running, nxt, flag), it + 1)

    st = lax.while_loop(cond, body, (jnp.where(all_sel, -3e38, rep(mn)), rep(hi0), rep(mx), n_adm_b,
                                     jnp.zeros((RW, LANES), f32), jnp.where(all_sel, DONE, RUN), jnp.int32(0)))
    tau = st[0]
    tie = jnp.where(st[5] == TIED, 1.0, 0.0)

    @pl.when(jnp.max(tie) > 0.5)
    def _():
        tau1, tie1 = tau[:, 0:1], tie[:, 0:1]
        need = kf - count_gt(tau)[:, 0:1]
        r = lax.broadcasted_iota(jnp.int32, (CH, CH), 0)
        c_ = lax.broadcasted_iota(jnp.int32, (CH, CH), 1)
        tri = jnp.where(r <= c_, 1.0, 0.0).astype(bf16)

        def fix(c, run):
            s = S[c]
            eq = (s == tau1) & (tie1 > 0.5)
            eqf = jnp.where(eq, 1.0, 0.0)
            pref = jnp.dot(eqf.astype(bf16), tri, preferred_element_type=f32) + run
            S[c] = jnp.where(eq & (pref > need), -inf, s)
            return run + jnp.sum(eqf, axis=1, keepdims=True)

        lax.fori_loop(0, n_ch, fix, jnp.zeros((RW, 1), f32))

    return tau


def _attn_prompt_kernel(qi_ref, wi_ref, kit_ref, q_ref, kt_ref, v_ref, o_ref, S, *, topk):
    qb = pl.program_id(1)
    QB = q_ref.shape[1]
    AB = min(QB, ATTN_ROWS)
    CH = S.shape[2]
    n_ch = (qb * QB + QB + CH - 1) // CH
    qpos = qb * QB + lax.broadcasted_iota(jnp.int32, (QB, 1), 0)
    wsc = wi_ref[0] * ((IDX_HEADS * IDX_DIM) ** -0.5)
    subs = [slice(r0, r0 + AB) for r0 in range(0, QB, AB)]
    qis = [jnp.concatenate([qi_ref[0, rs, h * IDX_DIM:(h + 1) * IDX_DIM] for h in range(IDX_HEADS)], axis=0)
           for rs in subs]

    def score_chunk(c, carry):
        kpos = c * CH + lax.broadcasted_iota(jnp.int32, (1, CH), 1)
        for rs, qi in zip(subs, qis):
            s = jnp.dot(qi, kit_ref[0, c], preferred_element_type=f32)
            s = jnp.maximum(s, 0.0).reshape(IDX_HEADS, AB, CH)
            w = wsc[rs]
            sc = s[0] * w[:, 0:1]
            for h in range(1, IDX_HEADS):
                sc = sc + s[h] * w[:, h:h + 1]
            S[c, rs, :] = jnp.where(kpos <= qpos[rs], sc, -float("inf"))
        return carry

    lax.fori_loop(0, n_ch, score_chunk, 0)
    tau = _select_tau(S, n_ch, topk, (qpos + 1).astype(f32))

    G = N_HEADS // N_KV_HEADS
    for si, rs in enumerate(subs):
        taub = _tile_lanes(tau[rs], CH)
        n_ch_s = (qb * QB + (si + 1) * AB + CH - 1) // CH
        qgs = [jnp.concatenate([q_ref[0, rs, (g * G + j) * HEAD_DIM:(g * G + j + 1) * HEAD_DIM]
                                for j in range(G)], axis=0) for g in range(N_KV_HEADS)]

        def chunk(c, carry, rs=rs, taub=taub, qgs=qgs):
            bias = jnp.where(S[c, rs, :] >= taub, 0.0, NEG)[None]
            off = pl.multiple_of(c * CH, CH)
            out = []
            for g in range(N_KV_HEADS):
                m, acc = carry[g]
                s = jnp.dot(qgs[g], kt_ref[0, c, g], preferred_element_type=f32)
                s = (s.reshape(G, AB, CH) + bias).reshape(G * AB, CH)
                m_new = jnp.maximum(m, jnp.max(s, axis=1, keepdims=True))
                p = jnp.exp(s - m_new).astype(bf16)
                acc = jnp.exp(m - m_new) * acc + jnp.dot(p, v_ref[0, g, pl.ds(off, CH), :],
                                                          preferred_element_type=f32)
                out.append((m_new, acc))
            return tuple(out)

        init = tuple((jnp.full((G * AB, 1), NEG, f32), jnp.zeros((G * AB, LANES), f32))
                     for _ in range(N_KV_HEADS))
        res = lax.fori_loop(0, n_ch_s, chunk, init)
        for g in range(N_KV_HEADS):
            acc = res[g][1]
            o = acc[:, :HEAD_DIM] / acc[:, HEAD_DIM:HEAD_DIM + 1]
            for j in range(G):
                h = g * G + j
                o_ref[0, rs, h * HEAD_DIM:(h + 1) * HEAD_DIM] = o[j * AB:(j + 1) * AB]


def _attn_prompt(q, k, v, qi, ki, wi):
    N, T, D = q.shape
    QB = min(Q_ROWS, T)
    topk = min(TOPK_MAX, T // 4)
    CH = min(KEY_CHUNK, T)
    NC = T // CH
    kit = ki.astype(bf16).reshape(N, NC, CH, IDX_DIM).transpose(0, 1, 3, 2)
    kt = k.astype(bf16).reshape(N, NC, CH, N_KV_HEADS, HEAD_DIM).transpose(0, 1, 3, 4, 2)
    vh = v.astype(bf16).reshape(N, T, N_KV_HEADS, HEAD_DIM).transpose(0, 2, 1, 3)
    vh = jnp.concatenate([vh, jnp.ones(vh.shape[:3] + (1,), bf16),
                          jnp.zeros(vh.shape[:3] + (LANES - HEAD_DIM - 1,), bf16)], axis=3)
    return pl.pallas_call(
        functools.partial(_attn_prompt_kernel, topk=topk),
        grid=(N, T // QB),
        in_specs=[pl.BlockSpec((1, QB, IDX_HEADS * IDX_DIM), lambda n, b: (n, b, 0)),
                  pl.BlockSpec((1, QB, IDX_HEADS), lambda n, b: (n, b, 0)),
                  pl.BlockSpec((1, NC, IDX_DIM, CH), lambda n, b: (n, 0, 0, 0)),
                  pl.BlockSpec((1, QB, D), lambda n, b: (n, b, 0)),
                  pl.BlockSpec((1, NC, N_KV_HEADS, HEAD_DIM, CH), lambda n, b: (n, 0, 0, 0, 0)),
                  pl.BlockSpec((1, N_KV_HEADS, T, LANES), lambda n, b: (n, 0, 0, 0))],
        out_specs=pl.BlockSpec((1, QB, D), lambda n, b: (n, b, 0)),
        out_shape=jax.ShapeDtypeStruct((N, T, D), f32),
        scratch_shapes=[pltpu.VMEM((NC, QB, CH), f32)],
        compiler_params=_cparams(("parallel", "arbitrary")),
        name="attn_prompt",
    )(qi, wi, kit, q, kt, vh)


def _sample_scores_kernel(pt_ref, qi_ref, w_ref, kin_ref, *rest, pp):
    pages = rest[:pp]
    sp_ref, sn_ref = rest[pp:]
    TQ = sp_ref.shape[2]
    qi = qi_ref[0]
    w = w_ref[0] * ((IDX_HEADS * IDX_DIM) ** -0.5)
    nt = (((1,), (1,)), ((), ()))

    def scores(s):
        s = jnp.maximum(s, 0.0) * w
        return jnp.sum(s.reshape(TQ, IDX_HEADS, s.shape[1]), axis=1)

    keys_t = jnp.concatenate([p[0].astype(bf16) for p in pages], axis=1)
    sp_ref[0, 0] = scores(jnp.dot(qi, keys_t, preferred_element_type=f32))

    @pl.when(pl.program_id(1) == 0)
    def _():
        s = scores(lax.dot_general(qi, kin_ref[0], nt, preferred_element_type=f32))
        t = lax.broadcasted_iota(jnp.int32, s.shape, 0)
        j = lax.broadcasted_iota(jnp.int32, s.shape, 1)
        sn_ref[0] = jnp.where(j <= t, s, -float("inf"))


def _sample_select_kernel(sp_ref, sn_ref, bias_ref, S, *, topk, past, tq):
    NCP = sp_ref.shape[0]
    RW, CH = S.shape[1], S.shape[2]
    for c in range(NCP):
        S[c] = sp_ref[c]
    S[NCP] = jnp.concatenate([sn_ref[...], jnp.full((RW, CH - LANES), -float("inf"), f32)], axis=1)
    t = lax.broadcasted_iota(jnp.int32, (RW, 1), 0) % tq
    taub = _tile_lanes(_select_tau(S, NCP + 1, topk, (past + 1 + t).astype(f32)), CH)
    for c in range(NCP + 1):
        bias_ref[c] = jnp.where(S[c] >= taub, 0.0, NEG)


def _sample_attend_kernel(pt_ref, q_ref, bp_ref, bn_ref, kn_ref, vn_ref, *rest, pp):
    kp = rest[:pp]
    vp = rest[pp:2 * pp]
    o_ref, m_ref, l_ref, acc_ref = rest[2 * pp:]
    p_id = pl.program_id(1)
    q = q_ref[0]
    R = q.shape[0]
    TQ = bp_ref.shape[2]
    nt = (((1,), (1,)), ((), ()))

    @pl.when(p_id == 0)
    def _():
        m_ref[...] = jnp.full(m_ref.shape, NEG, f32)
        l_ref[...] = jnp.zeros(l_ref.shape, f32)
        acc_ref[...] = jnp.zeros(acc_ref.shape, f32)

    def step(s, pv, bias):
        L = s.shape[1]
        b = jnp.broadcast_to(bias[:, None, :], (TQ, R // TQ, L)).reshape(R, L)
        s = jnp.where(b < 0.0, NEG, s)
        m = m_ref[...]
        m_new = jnp.maximum(m, jnp.max(s, axis=1, keepdims=True))
        p = jnp.exp(s - m_new)
        a = jnp.exp(m - m_new)
        l_ref[...] = a * l_ref[...] + jnp.sum(p, axis=1, keepdims=True)
        acc_ref[...] = a * acc_ref[...] + pv(p.astype(bf16))
        m_ref[...] = m_new

    keys_t = jnp.concatenate([p[0].astype(bf16) for p in kp], axis=1)
    vals_t = jnp.concatenate([p[0].astype(bf16) for p in vp], axis=1)
    step(jnp.dot(q, keys_t, preferred_element_type=f32),
         lambda p: lax.dot_general(p, vals_t, nt, preferred_element_type=f32), bp_ref[0, 0])

    @pl.when(p_id == pl.num_programs(1) - 1)
    def _():
        step(lax.dot_general(q, kn_ref[0], nt, preferred_element_type=f32),
             lambda p: jnp.dot(p, vn_ref[0], preferred_element_type=f32), bn_ref[0])
        o_ref[0] = acc_ref[...] / l_ref[...]


def _attn_sample(q, k_new, v_new, qi, ki_new, wi, cache_k, cache_v, cache_kidx, page_table):
    N, tq, _ = q.shape
    n_pages = page_table.shape[1]
    past = n_pages * PAGE
    topk = min(TOPK_MAX, (past + tq) // 4)
    PP = math.gcd(PAGES_PER_STEP, n_pages)
    NP = n_pages // PP
    LP = PP * PAGE
    n_pool = cache_k.shape[0]
    KV = N_KV_HEADS * HEAD_DIM

    qi2 = qi.reshape(N, tq * IDX_HEADS, IDX_DIM)
    w2 = wi.reshape(N, tq * IDX_HEADS, 1)
    pad_rows = lambda a: jnp.pad(a, ((0, 0), (0, LANES - tq), (0, 0)))
    kin = pad_rows(ki_new.astype(bf16))

    def page_spec(shape, j):
        return pl.BlockSpec(shape, lambda n, p, pt: (pt[n, p * PP + j], 0, 0))

    sp, sn = pl.pallas_call(
        functools.partial(_sample_scores_kernel, pp=PP),
        grid_spec=pltpu.PrefetchScalarGridSpec(
            num_scalar_prefetch=1, grid=(N, NP),
            in_specs=[pl.BlockSpec((1, tq * IDX_HEADS, IDX_DIM), lambda n, p, pt: (n, 0, 0)),
                      pl.BlockSpec((1, tq * IDX_HEADS, 1), lambda n, p, pt: (n, 0, 0)),
                      pl.BlockSpec((1, LANES, IDX_DIM), lambda n, p, pt: (n, 0, 0))]
                     + [page_spec((1, IDX_DIM, PAGE), j) for j in range(PP)],
            out_specs=[pl.BlockSpec((1, 1, tq, LP), lambda n, p, pt: (n, p, 0, 0)),
                       pl.BlockSpec((1, tq, LANES), lambda n, p, pt: (n, 0, 0))]),
        out_shape=[jax.ShapeDtypeStruct((N, NP, tq, LP), f32), jax.ShapeDtypeStruct((N, tq, LANES), f32)],
        compiler_params=_cparams(("parallel", "arbitrary")),
        name="sample_scores",
    )(page_table, qi2, w2, kin, *([cache_kidx.transpose(0, 2, 1)] * PP))

    CH = KEY_CHUNK
    NCP = past // CH
    RW = N * tq
    sp2 = sp.transpose(0, 2, 1, 3).reshape(RW, NCP, CH).transpose(1, 0, 2)
    bias = pl.pallas_call(
        functools.partial(_sample_select_kernel, topk=topk, past=past, tq=tq),
        out_shape=jax.ShapeDtypeStruct((NCP + 1, RW, CH), f32),
        scratch_shapes=[pltpu.VMEM((NCP + 1, RW, CH), f32)],
        compiler_params=pltpu.CompilerParams(vmem_limit_bytes=VMEM_LIMIT),
        name="sample_select",
    )(sp2, sn.reshape(RW, LANES))
    bp = bias[:NCP].transpose(1, 0, 2).reshape(N, tq, NP, LP).transpose(0, 2, 1, 3)
    bn = bias[NCP, :, :LANES].reshape(N, tq, LANES)

    G = N_HEADS // N_KV_HEADS
    q5 = q.reshape(N, tq, N_KV_HEADS, G, 1, HEAD_DIM)
    eye = jnp.eye(N_KV_HEADS, dtype=q.dtype).reshape(1, 1, N_KV_HEADS, 1, N_KV_HEADS, 1)
    qbd = (q5 * eye).reshape(N, tq * N_HEADS, KV)
    kn = pad_rows(k_new.astype(bf16))
    vn = pad_rows(v_new.astype(bf16))
    ck = cache_k.transpose(0, 2, 3, 1).reshape(n_pool, KV, PAGE)
    cv = cache_v.transpose(0, 2, 3, 1).reshape(n_pool, KV, PAGE)
    R = tq * N_HEADS
    o = pl.pallas_call(
        functools.partial(_sample_attend_kernel, pp=PP),
        grid_spec=pltpu.PrefetchScalarGridSpec(
            num_scalar_prefetch=1, grid=(N, NP),
            in_specs=[pl.BlockSpec((1, R, KV), lambda n, p, pt: (n, 0, 0)),
                      pl.BlockSpec((1, 1, tq, LP), lambda n, p, pt: (n, p, 0, 0)),
                      pl.BlockSpec((1, tq, LANES), lambda n, p, pt: (n, 0, 0)),
                      pl.BlockSpec((1, LANES, KV), lambda n, p, pt: (n, 0, 0)),
                      pl.BlockSpec((1, LANES, KV), lambda n, p, pt: (n, 0, 0))]
                     + [page_spec((1, KV, PAGE), j) for j in range(PP)]
                     + [page_spec((1, KV, PAGE), j) for j in range(PP)],
            out_specs=pl.BlockSpec((1, R, KV), lambda n, p, pt: (n, 0, 0)),
            scratch_shapes=[pltpu.VMEM((R, 1), f32), pltpu.VMEM((R, 1), f32), pltpu.VMEM((R, KV), f32)]),
        out_shape=jax.ShapeDtypeStruct((N, R, KV), f32),
        compiler_params=_cparams(("parallel", "arbitrary")),
        name="sample_attend",
    )(page_table, qbd, bp, bn, kn, vn, *([ck] * PP), *([cv] * PP))
    o6 = o.reshape(N, tq, N_KV_HEADS, G, N_KV_HEADS, HEAD_DIM)
    sel = jnp.eye(N_KV_HEADS, dtype=f32).reshape(1, 1, N_KV_HEADS, 1, N_KV_HEADS, 1)
    return jnp.sum(o6 * sel, axis=4).reshape(N, tq, N_HEADS * HEAD_DIM)


HALO = 32


def _conv_kernel(u_ref, prev_ref, cw_ref, cb_ref, g_ref, b_ref, wo_ref, o_ref, buf):
    tt = u_ref.shape[1]

    @pl.when(pl.program_id(1) == 0)
    def _():
        buf[0:HALO] = prev_ref[0]

    @pl.when(pl.program_id(1) > 0)
    def _():
        buf[0:HALO] = buf[tt:tt + HALO]

    buf[HALO:HALO + tt] = u_ref[0]
    off = HALO - (D_CONV_W - 1)
    y = buf[off:off + tt] * cw_ref[0:1, :]
    for j in range(1, D_CONV_W):
        y = y + buf[off + j:off + j + tt] * cw_ref[j:j + 1, :]
    y = y + cb_ref[...]
    mu = jnp.mean(y, axis=-1, keepdims=True)
    yc = y - mu
    var = jnp.mean(yc * yc, axis=-1, keepdims=True)
    y = yc * lax.rsqrt(var + LN_EPS) * g_ref[...] + b_ref[...]
    y = y * jax.nn.sigmoid(y)
    o_ref[0] = jnp.dot(y.astype(bf16), wo_ref[...], preferred_element_type=f32)


def _conv_module(u, prev, conv_w, conv_b, ln_g, ln_b, w_conv_out_bf, tt):
    N, T, C = u.shape
    D = w_conv_out_bf.shape[1]
    prev_pad = jnp.pad(prev, ((0, 0), (HALO - prev.shape[1], 0), (0, 0)))
    cw = jnp.pad(conv_w, ((0, 32 - conv_w.shape[0]), (0, 0)))
    row2 = lambda a: a.reshape(1, -1)
    return pl.pallas_call(
        _conv_kernel,
        grid=(N, T // tt),
        in_specs=[pl.BlockSpec((1, tt, C), lambda n, t: (n, t, 0)),
                  pl.BlockSpec((1, HALO, C), lambda n, t: (n, 0, 0)),
                  pl.BlockSpec((32, C), lambda n, t: (0, 0)),
                  pl.BlockSpec((1, C), lambda n, t: (0, 0)),
                  pl.BlockSpec((1, C), lambda n, t: (0, 0)),
                  pl.BlockSpec((1, C), lambda n, t: (0, 0)),
                  pl.BlockSpec((C, D), lambda n, t: (0, 0))],
        out_specs=pl.BlockSpec((1, tt, D), lambda n, t: (n, t, 0)),
        out_shape=jax.ShapeDtypeStruct((N, T, D), f32),
        scratch_shapes=[pltpu.VMEM((HALO + tt + 8, C), f32)],
        compiler_params=_cparams(("parallel", "arbitrary")),
        name="conv_module",
    )(u, prev_pad, cw, row2(conv_b), row2(ln_g), row2(ln_b), w_conv_out_bf)


def _merge_kernel(x_ref, co_ref, ao_ref, gc_ref, ga_ref, wo_ref, g_ref, b_ref, ht_ref, htb_ref):
    m = gc_ref[...] * co_ref[...] + ga_ref[...] * ao_ref[...]
    z = ALPHA * x_ref[...] + jnp.dot(m.astype(bf16), wo_ref[...], preferred_element_type=f32)
    mu = jnp.mean(z, axis=-1, keepdims=True)
    zc = z - mu
    var = jnp.mean(zc * zc, axis=-1, keepdims=True)
    ht = (zc * lax.rsqrt(var + LN_EPS) * g_ref[...] + b_ref[...]).T
    ht_ref[...] = ht
    htb_ref[...] = ht.astype(bf16)


def _merge(x2, conv_o, attn_o, sgc, sga, w_out_bf, ln_g, ln_b, tm):
    R, D = x2.shape
    row = lambda i: (i, 0)
    cst = lambda i: (0, 0)
    col = lambda i: (0, i)
    return pl.pallas_call(
        _merge_kernel,
        grid=(R // tm,),
        in_specs=[pl.BlockSpec((tm, D), row)] * 5
                 + [pl.BlockSpec((D, D), cst), pl.BlockSpec((1, D), cst), pl.BlockSpec((1, D), cst)],
        out_specs=[pl.BlockSpec((D, tm), col), pl.BlockSpec((D, tm), col)],
        out_shape=[jax.ShapeDtypeStruct((D, R), f32), jax.ShapeDtypeStruct((D, R), bf16)],
        compiler_params=_cparams(("parallel",)),
        name="merge",
    )(x2, conv_o, attn_o, sgc, sga, w_out_bf, ln_g.reshape(1, D), ln_b.reshape(1, D))


def _extract16(s, idx, exact):
    rank = jnp.full(s.shape, float(PEER_TOPK), f32)
    tops = []
    for r in range(PEER_TOPK):
        m = jnp.max(s, axis=0, keepdims=True)
        if exact:
            pick = idx == jnp.min(jnp.where(s == m, idx, 1e9), axis=0, keepdims=True)
        else:
            pick = s == m
        rank = jnp.where(pick, float(r), rank)
        s = jnp.where(pick, -float("inf"), s)
        tops.append(m)
    return rank, jnp.concatenate(tops, axis=0)


def _route_head(s1, s2, exact):
    K, Tn = s1.shape
    T16 = float(PEER_TOPK)
    ninf = -float("inf")
    rows = lax.broadcasted_iota(jnp.int32, (K, Tn), 0).astype(f32)
    sub = lax.broadcasted_iota(jnp.int32, (8, Tn), 0)
    subf = sub.astype(f32)
    r1, t1 = _extract16(s1, rows, exact)
    r2, t2 = _extract16(s2, rows, exact)
    limits = [16, 8, 5, 4, 3, 2, 2, 2]
    cands, idxs = [], []
    cands.append(t1[0:1] + t2[0:8]); idxs.append(subf)
    cands.append(t1[0:1] + t2[8:16]); idxs.append(subf + 8.0)
    for a in range(1, 8):
        cands.append(jnp.where(sub < limits[a], t1[a:a + 1] + t2[0:8], ninf))
        idxs.append(subf + float(16 * a))
    cands.append(t1[8:16] + t2[0:1]); idxs.append((subf + 8.0) * 16.0)
    cand = jnp.concatenate(cands, axis=0)
    cidx = jnp.concatenate(idxs, axis=0)
    ecand = jnp.exp(cand - cand[0:1])
    rc, _ = _extract16(cand, cidx, exact)
    picked = jnp.where(rc < T16, 1.0, 0.0)
    z = jnp.sum(picked * ecand, axis=0, keepdims=True)
    la = [jnp.sum(picked[0:16], axis=0, keepdims=True)]
    for a in range(1, 8):
        la.append(jnp.sum(picked[8 + 8 * a:16 + 8 * a], axis=0, keepdims=True))
    ltail = picked[72:80]
    c = jnp.zeros(s1.shape, f32)
    for a in range(8):
        c = jnp.where(r1 == float(a), la[a], c)
    for a in range(8, 16):
        c = jnp.where(r1 == float(a), ltail[a - 8:a - 7], c)
    in1 = r1 < T16
    in2 = r2 < T16
    e1 = jnp.where(in1, jnp.exp(s1 - t1[0:1]), 0.0) / z
    e2 = jnp.where(in2, jnp.exp(s2 - t2[0:1]), 0.0)
    n1 = jnp.sum(jnp.where(in1, 1.0, 0.0), axis=0, keepdims=True)
    n2 = jnp.sum(jnp.where(in2, 1.0, 0.0), axis=0, keepdims=True)
    nc = jnp.sum(picked, axis=0, keepdims=True)
    ok = (n1 == T16) & (n2 == T16) & (nc == T16)
    return r2, e2, c, e1, ok


def _route_kernel(ht_ref, wq_ref, k1_ref, k2_ref, r2_ref, e2_ref, c_ref, e1_ref):
    qh = jnp.dot(wq_ref[...], ht_ref[...], preferred_element_type=f32)
    half = PEER_DQ // 2

    def store(h, r2, e2, c, e1):
        r2_ref[h] = r2.astype(bf16)
        e2_ref[h] = e2.astype(bf16)
        c_ref[h] = c
        e1_ref[h] = e1

    for h in range(PEER_HEADS):
        q1 = qh[h * PEER_DQ:h * PEER_DQ + half].astype(bf16)
        q2 = qh[h * PEER_DQ + half:(h + 1) * PEER_DQ].astype(bf16)
        s1 = jnp.dot(k1_ref[h], q1, preferred_element_type=f32)
        s2 = jnp.dot(k2_ref[h], q2, preferred_element_type=f32)
        r2, e2, c, e1, ok = _route_head(s1, s2, exact=False)
        store(h, r2, e2, c, e1)

        @pl.when(jnp.min(jnp.where(ok, 1.0, 0.0)) < 0.5)
        def _(h=h, s1=s1, s2=s2):
            store(h, *_route_head(s1, s2, exact=True)[:4])


def _route(ht_bf, wqt_bf, k1_bf, k2_bf, tn):
    D, R = ht_bf.shape
    shp = lambda dt: jax.ShapeDtypeStruct((PEER_HEADS, PEER_NKEYS, R), dt)
    ospec = pl.BlockSpec((PEER_HEADS, PEER_NKEYS, tn), lambda i: (0, 0, i))
    return pl.pallas_call(
        _route_kernel,
        grid=(R // tn,),
        in_specs=[pl.BlockSpec((D, tn), lambda i: (0, i)),
                  pl.BlockSpec(wqt_bf.shape, lambda i: (0, 0)),
                  pl.BlockSpec(k1_bf.shape, lambda i: (0, 0, 0)),
                  pl.BlockSpec(k2_bf.shape, lambda i: (0, 0, 0))],
        out_specs=[ospec] * 4,
        out_shape=[shp(bf16), shp(bf16), shp(f32), shp(f32)],
        compiler_params=_cparams(("parallel",)),
        name="peer_route",
    )(ht_bf, wqt_bf, k1_bf, k2_bf)


ET = 2048
PEER_SUBTILE = 256


def _peer_kernel(htb_ref, ht_ref, u_ref, vt_ref, r2_ref, e2_ref, c_ref, e1_ref, g_ref, b_ref,
                 y_ref, acc_ref, p_ref):
    et = pl.program_id(1)
    NK = PEER_NKEYS

    @pl.when(et == 0)
    def _():
        acc_ref[...] = jnp.zeros(acc_ref.shape, f32)

    Tn = htb_ref.shape[1]
    SL = 16
    TS = p_ref.shape[2]
    zero = jnp.zeros((), bf16)
    acts = [[jnp.dot(u_ref[ii * NK:(ii + 1) * NK, :], htb_ref[:, ts * TS:(ts + 1) * TS],
                     preferred_element_type=f32) for ii in range(ET // NK)] for ts in range(Tn // TS)]
    for ts in range(Tn // TS):
        tok = slice(ts * TS, (ts + 1) * TS)
        for ii in range(ET // NK):
            a = acts[ts][ii]
            gl = (0.5 * a * (1.0 + lax.erf(a * (2.0 ** -0.5)))).astype(bf16)
            w = [jnp.zeros((SL, TS), bf16) for _ in range(NK // SL)]
            for h in range(PEER_HEADS):
                cb = jnp.broadcast_to(c_ref[h, ii:ii + 1, tok], (SL, TS)).astype(bf16)
                eb = jnp.broadcast_to(e1_ref[h, ii:ii + 1, tok], (SL, TS)).astype(bf16)
                for s in range(NK // SL):
                    rows = slice(s * SL, (s + 1) * SL)
                    w[s] = w[s] + jnp.where(r2_ref[h, rows, tok] < cb, eb * e2_ref[h, rows, tok], zero)
            for s in range(NK // SL):
                p_ref[ts, ii * NK + s * SL:ii * NK + (s + 1) * SL, :] = w[s] * gl[s * SL:(s + 1) * SL]
        acc_ref[:, tok] += jnp.dot(vt_ref[...], p_ref[ts], preferred_element_type=f32)

    @pl.when(et == pl.num_programs(1) - 1)
    def _():
        z = ALPHA * ht_ref[...] + acc_ref[...]
        mu = jnp.mean(z, axis=0, keepdims=True)
        zc = z - mu
        var = jnp.mean(zc * zc, axis=0, keepdims=True)
        y_ref[...] = (zc * lax.rsqrt(var + LN_EPS) * g_ref[...] + b_ref[...]).T


def _peer_dense(ht_bf, ht, u_bf, vt_bf, r2, e2, c, e1, ln_g, ln_b, tn):
    D, R = ht.shape
    E = u_bf.shape[0]
    NI = ET // PEER_NKEYS
    ts = min(tn, PEER_SUBTILE)
    tok = lambda i, e: (0, i)
    rt = pl.BlockSpec((PEER_HEADS, PEER_NKEYS, tn), lambda i, e: (0, 0, i))
    ri = pl.BlockSpec((PEER_HEADS, NI, tn), lambda i, e: (0, e, i))
    return pl.pallas_call(
        _peer_kernel,
        grid=(R // tn, E // ET),
        in_specs=[pl.BlockSpec((D, tn), tok), pl.BlockSpec((D, tn), tok),
                  pl.BlockSpec((ET, D), lambda i, e: (e, 0)),
                  pl.BlockSpec((D, ET), lambda i, e: (0, e)),
                  rt, rt, ri, ri,
                  pl.BlockSpec((D, 1), lambda i, e: (0, 0)), pl.BlockSpec((D, 1), lambda i, e: (0, 0))],
        out_specs=pl.BlockSpec((tn, D), lambda i, e: (i, 0)),
        out_shape=jax.ShapeDtypeStruct((R, D), f32),
        scratch_shapes=[pltpu.VMEM((D, tn), f32), pltpu.VMEM((tn // ts, ET, ts), bf16)],
        compiler_params=_cparams(("parallel", "arbitrary")),
        name="peer_dense",
    )(ht_bf, ht, u_bf, vt_bf, r2, e2, c, e1, ln_g.reshape(D, 1), ln_b.reshape(D, 1))


def _channel_mix(ht, ht_bf, wqt_bf, k1_bf, k2_bf, u_bf, vt_bf, ln_g, ln_b, tn_route, tn_dense):
    r2, e2, c, e1 = _route(ht_bf, wqt_bf, k1_bf, k2_bf, tn_route)
    return _peer_dense(ht_bf, ht, u_bf, vt_bf, r2, e2, c, e1, ln_g, ln_b, tn_dense)


def _prep_weights(w_in, w_conv_out, w_out, peer_w_query, peer_keys1, peer_keys2, peer_u, peer_v):
    D = w_in.shape[0]
    split = 2 * (D // 2) + N_HEADS * HEAD_DIM + 2 * N_KV_HEADS * HEAD_DIM + IDX_HEADS * IDX_DIM + IDX_DIM + IDX_HEADS
    pad = (-split) % LANES
    w_pad = jnp.concatenate([w_in[:, :split], jnp.zeros((D, pad), w_in.dtype), w_in[:, split:]], axis=1).astype(bf16)
    wqt = peer_w_query.reshape(D, PEER_HEADS * PEER_DQ).T.astype(bf16)
    return dict(w_pad=w_pad, wco=w_conv_out.astype(bf16), wo=w_out.astype(bf16), wqt=wqt,
                k1=peer_keys1.astype(bf16), k2=peer_keys2.astype(bf16),
                u=peer_u.astype(bf16), vt=peer_v.T.astype(bf16))


def _group(x, pos, prev, W, conv_w, conv_b, conv_ln_g, conv_ln_b, ln1_g, ln1_b, ln2_g, ln2_b,
           attn_fn, tm, tt, tn_route, tn_dense):
    N, T, D = x.shape
    R = N * T
    x2 = x.reshape(R, D)
    tabs = _rope_tables(pos if T % tm == 0 else jnp.tile(pos, tm // T))
    u, q, k, v, qi, ki, wi, sgc, sga = _project(x2, W["w_pad"], tabs, tm)
    r3 = lambda a: a.reshape(N, T, a.shape[-1])
    attn_o = attn_fn(r3(q), r3(k), r3(v), r3(qi), r3(ki), r3(wi))
    u3 = r3(u)
    if T % tt == 0:
        conv_o = _conv_module(u3, prev, conv_w, conv_b, conv_ln_g, conv_ln_b, W["wco"], tt)
    else:
        up = jnp.pad(u3, ((0, 0), (0, tt - T), (0, 0)))
        conv_o = _conv_module(up, prev, conv_w, conv_b, conv_ln_g, conv_ln_b, W["wco"], tt)[:, :T]
    ht, ht_bf = _merge(x2, conv_o.reshape(R, D), attn_o.reshape(R, D), sgc, sga, W["wo"], ln1_g, ln1_b, tm)
    y2 = _channel_mix(ht, ht_bf, W["wqt"], W["k1"], W["k2"], W["u"], W["vt"], ln2_g, ln2_b, tn_route, tn_dense)
    return y2.reshape(N, T, D), k, v, ki, u3


def kernel(x_prompt, x_sample, cache_k, cache_v, cache_kidx, state_conv, page_table, w_in, conv_w, conv_b,
           conv_ln_g, conv_ln_b, w_conv_out, w_out, ln1_g, ln1_b, peer_w_query, peer_keys1, peer_keys2,
           peer_u, peer_v, ln2_g, ln2_b):
    W = _prep_weights(w_in, w_conv_out, w_out, peer_w_query, peer_keys1, peer_keys2, peer_u, peer_v)
    common = (W, conv_w, conv_b, conv_ln_g, conv_ln_b, ln1_g, ln1_b, ln2_g, ln2_b)
    C = conv_w.shape[1]
    keep = D_CONV_W - 1

    N, T, D = x_prompt.shape
    tm = min(256, N * T)
    yp, k_p, v_p, kidx_p, u_p = _group(
        x_prompt, jnp.arange(T, dtype=jnp.int32), jnp.zeros((N, keep, C), f32), *common,
        _attn_prompt, tm, min(512, T), min(256, N * T), min(512, N * T))
    conv_p = u_p[:, T - keep:]

    NS, tq, _ = x_sample.shape
    past = page_table.shape[1] * PAGE
    attn_s = functools.partial(_attn_sample, cache_k=cache_k, cache_v=cache_v, cache_kidx=cache_kidx,
                               page_table=page_table)
    rs = NS * tq
    ys, k_s, v_s, kidx_s, u_s = _group(
        x_sample, past + jnp.arange(tq, dtype=jnp.int32), state_conv, *common,
        lambda q, k, v, qi, ki, wi: attn_s(q, k, v, qi, ki, wi), rs, 8, rs, rs)
    conv_s = jnp.concatenate([state_conv, u_s], axis=1)[:, -keep:]

    kv4 = lambda a, n, t: a.reshape(n, t, N_KV_HEADS, HEAD_DIM)
    return (yp, ys, kv4(k_p, N, T), kv4(v_p, N, T), kidx_p.reshape(N, T, IDX_DIM), conv_p,
            kv4(k_s, NS, tq), kv4(v_s, NS, tq), kidx_s.reshape(NS, tq, IDX_DIM), conv_s)
```

```python
import functools
import math

import jax
import jax.numpy as jnp
from jax import lax
from jax.experimental import pallas as pl
from jax.experimental.pallas import tpu as pltpu

f32 = jnp.float32
bf16 = jnp.bfloat16

D_CONV_W = 31
N_HEADS = 16
N_KV_HEADS = 4
HEAD_DIM = 64
IDX_HEADS = 8
IDX_DIM = 64
TOPK_MAX = 256
ROPE_THETA = 500000.0
ROPE_HALF = 8
PAGE = 128
PEER_HEADS = 8
PEER_NKEYS = 128
PEER_DQ = 128
PEER_TOPK = 16
ALPHA = 2.0 ** 0.25
LN_EPS = 1e-5

LANES = 128
VMEM_LIMIT = 56 * 1024 * 1024
NEG = -1e30
KEY_CHUNK = 512
Q_ROWS = 256
ATTN_ROWS = 256
PAGES_PER_STEP = 16


def _cparams(sem):
    return pltpu.CompilerParams(dimension_semantics=sem, vmem_limit_bytes=VMEM_LIMIT)


def _rope_tables(pos):
    inv = jnp.power(ROPE_THETA, -jnp.arange(ROPE_HALF, dtype=f32) / ROPE_HALF)
    ang = pos.astype(f32)[:, None] * inv
    cos, sin = jnp.cos(ang), jnp.sin(ang)
    d = jnp.arange(HEAD_DIM)
    cosp = jnp.where(d < 2 * ROPE_HALF, cos[:, d % ROPE_HALF], 1.0)
    sap = jnp.where(d < ROPE_HALF, -sin[:, d % ROPE_HALF], 0.0)
    sbp = jnp.where((d >= ROPE_HALF) & (d < 2 * ROPE_HALF), sin[:, d % ROPE_HALF], 0.0)
    one = jnp.ones_like(cosp)
    zero = jnp.zeros_like(cosp)
    return (jnp.concatenate([cosp, cosp, cosp, one], axis=1),
            jnp.concatenate([sap, sap, sap, zero], axis=1),
            jnp.concatenate([sbp, sbp, sbp, zero], axis=1))


def _rope(z, cos, sa, sb):
    outs = []
    for c in range(z.shape[1] // LANES):
        zc = z[:, c * LANES:(c + 1) * LANES]
        outs.append(zc * cos + pltpu.roll(zc, LANES - ROPE_HALF, 1) * sa + pltpu.roll(zc, ROPE_HALF, 1) * sb)
    return outs[0] if len(outs) == 1 else jnp.concatenate(outs, axis=1)


def _proj_kernel(x_ref, w_ref, cos_ref, sa_ref, sb_ref,
                 u_ref, q_ref, k_ref, v_ref, qi_ref, ki_ref, wi_ref, gc_ref, ga_ref):
    x = x_ref[...].astype(bf16)
    cos, sa, sb = cos_ref[:, :LANES], sa_ref[:, :LANES], sb_ref[:, :LANES]
    cos2, sa2, sb2 = cos_ref[:, LANES:], sa_ref[:, LANES:], sb_ref[:, LANES:]

    def mm(c0, c1):
        return jnp.dot(x, w_ref[:, c0:c1], preferred_element_type=f32)

    z = mm(0, 1024)
    u_ref[...] = z[:, :512] * jax.nn.sigmoid(z[:, 512:])
    z = mm(1024, 2048)
    q_ref[...] = (_rope(z, cos, sa, sb) * (HEAD_DIM ** -0.5)).astype(bf16)
    z = mm(2048, 2560)
    k_ref[...] = _rope(z[:, :256], cos, sa, sb)
    v_ref[...] = z[:, 256:]
    z = mm(2560, 3072)
    qi_ref[...] = _rope(z, cos, sa, sb).astype(bf16)
    z = mm(3072, 3200)
    z = _rope(z, cos2, sa2, sb2)
    ki_ref[...] = z[:, :IDX_DIM]
    wi_ref[...] = z[:, IDX_DIM:IDX_DIM + IDX_HEADS]
    gc_ref[...] = jax.nn.sigmoid(mm(3200, 4224))
    ga_ref[...] = jax.nn.sigmoid(mm(4224, 5248))


def _project(x2, w_pad, tabs, tm):
    R, D = x2.shape
    rt = tabs[0].shape[0]
    nt = rt // tm
    row = lambda i: (i, 0)
    tab = lambda i: (i % nt, 0)
    widths = [(512, f32), (1024, bf16), (256, f32), (256, f32), (512, bf16), (IDX_DIM, f32), (IDX_HEADS, f32),
              (1024, f32), (1024, f32)]
    return pl.pallas_call(
        _proj_kernel,
        grid=(R // tm,),
        in_specs=[pl.BlockSpec((tm, D), row),
                  pl.BlockSpec(w_pad.shape, lambda i: (0, 0)),
                  pl.BlockSpec((tm, 256), tab), pl.BlockSpec((tm, 256), tab), pl.BlockSpec((tm, 256), tab)],
        out_specs=[pl.BlockSpec((tm, w), row) for w, _ in widths],
        out_shape=[jax.ShapeDtypeStruct((R, w), dt) for w, dt in widths],
        compiler_params=_cparams(("parallel",)),
        name="proj",
    )(x2, w_pad, *tabs)


def _select_tau_t(ST, n_ch, k, n_adm):
    _, CH, RW = ST.shape
    kf = float(k)
    inf = float("inf")
    SUB = 8

    def fold(x):
        return x.reshape(CH // SUB, SUB, RW)

    def mm_body(c, carry):
        mn, mx = carry
        s = ST[c]
        mx = jnp.maximum(mx, jnp.max(fold(s), axis=0))
        mn = jnp.minimum(mn, jnp.min(fold(jnp.where(s == -inf, inf, s)), axis=0))
        return mn, mx

    mn, mx = lax.fori_loop(0, n_ch, mm_body, (jnp.full((SUB, RW), inf, f32), jnp.full((SUB, RW), -inf, f32)))
    mn = jnp.min(mn, axis=0, keepdims=True)
    mx = jnp.max(mx, axis=0, keepdims=True)
    hi0 = mx + jnp.maximum(jnp.abs(mx), 1e-30) * (2.0 ** -20)

    def count_gt(t):
        def body(c, acc):
            return acc + jnp.sum(fold(jnp.where(ST[c] > t, 1.0, 0.0)), axis=0)

        return jnp.sum(lax.fori_loop(0, n_ch, body, jnp.zeros((SUB, RW), f32)), axis=0, keepdims=True)

    def probe(t):
        def body(c, carry):
            acc, bm = carry
            s = ST[c]
            hit = s >= t
            acc = acc + jnp.sum(fold(jnp.where(hit, 1.0, 0.0)), axis=0)
            bm = jnp.maximum(bm, jnp.max(fold(jnp.where(hit, -inf, s)), axis=0))
            return acc, bm

        acc, bm = lax.fori_loop(0, n_ch, body, (jnp.zeros((SUB, RW), f32), jnp.full((SUB, RW), -inf, f32)))
        return jnp.sum(acc, axis=0, keepdims=True), jnp.max(bm, axis=0, keepdims=True)

    def to_key(x):
        b = lax.bitcast_convert_type(x, jnp.int32)
        return b ^ ((b >> 31) & 0x7FFFFFFF)

    all_sel = n_adm <= kf
    RUN, STALL, DONE, TIED = 0.0, 1.0, 2.0, 3.0

    def cond(st):
        return jnp.logical_and(jnp.min(st[5]) < DONE, st[6] < 400)

    def body(st):
        lo, hi, hb, cl, ch, flag, it = st

        def value_half():
            h = lo + (hi - lo) * 0.5
            return h, jnp.where(h <= lo, 1.0, jnp.where(h >= hi, 1.0, 0.0))

        def key_half():
            lk, hk = to_key(lo), to_key(hi)
            midk = (lk >> 1) + (hk >> 1) + (lk & hk & 1)
            h = lax.bitcast_convert_type(midk ^ ((midk >> 31) & 0x7FFFFFFF), f32)
            return h, jnp.where(midk == lk, 1.0, 0.0)

        half, adj = lax.cond(it >= 24, key_half, value_half)
        descf = jnp.where(cl - ch <= 4.0, 1.0, jnp.where(flag == STALL, 1.0, adj))
        mid = jnp.where(descf > 0.5, hb, half)
        cnt, b = probe(mid)
        ge = cnt >= kf
        fin = jnp.where(cnt == kf, 1.0, jnp.where(ge, descf, 0.0))
        idle = jnp.where(cnt == cl, STALL, jnp.where(cnt == ch, STALL, RUN))
        nxt = jnp.where(fin > 0.5, jnp.where(cnt > kf, TIED, DONE), idle)
        running = flag < DONE
        return (jnp.where(running, jnp.where(ge, mid, lo), lo), jnp.where(ge, hi, mid), jnp.where(ge, hb, b),
                jnp.where(ge, cnt, cl), jnp.where(ge, ch, cnt), jnp.where(running, nxt, flag), it + 1)

    st = lax.while_loop(cond, body, (jnp.where(all_sel, -3e38, mn), hi0, mx, n_adm, jnp.zeros((1, RW), f32),
                                     jnp.where(all_sel, DONE, RUN), jnp.int32(0)))
    tau = st[0]
    tie = st[5] == TIED

    @pl.when(jnp.max(st[5]) > DONE)
    def _():
        need = kf - count_gt(tau)
        r = lax.broadcasted_iota(jnp.int32, (CH, CH), 0)
        c_ = lax.broadcasted_iota(jnp.int32, (CH, CH), 1)
        tri = jnp.where(c_ <= r, 1.0, 0.0).astype(bf16)

        def fix(c, run):
            s = ST[c]
            eqf = jnp.where(tie, jnp.where(s == tau, 1.0, 0.0), 0.0)
            pref = jnp.dot(tri, eqf.astype(bf16), preferred_element_type=f32) + run
            ST[c] = jnp.where(eqf * pref > need, -inf, s)
            return run + jnp.sum(eqf, axis=0, keepdims=True)

        lax.fori_loop(0, n_ch, fix, jnp.zeros((1, RW), f32))

    return tau


def _attn_prompt_kernel(qit_ref, wit_ref, ki_ref, q_ref, kt_ref, v_ref, o_ref, ST, *, topk):
    qb = pl.program_id(1)
    QB = q_ref.shape[1]
    AB = min(QB, ATTN_ROWS)
    CH = ST.shape[1]
    n_ch = (qb * QB + QB + CH - 1) // CH
    qpos = qb * QB + lax.broadcasted_iota(jnp.int32, (1, QB), 1)
    wsc = wit_ref[0] * ((IDX_HEADS * IDX_DIM) ** -0.5)
    subs = [slice(r0, r0 + AB) for r0 in range(0, QB, AB)]
    qit = jnp.concatenate([qit_ref[0, h] for h in range(IDX_HEADS)], axis=1)

    def score_chunk(c, carry):
        s = jnp.dot(ki_ref[0, c], qit, preferred_element_type=f32)
        sc = jnp.maximum(s[:, 0:QB], 0.0) * wsc[0:1]
        for h in range(1, IDX_HEADS):
            sc = sc + jnp.maximum(s[:, h * QB:(h + 1) * QB], 0.0) * wsc[h:h + 1]
        kpos = c * CH + lax.broadcasted_iota(jnp.int32, (CH, 1), 0)
        ST[c] = jnp.where(kpos <= qpos, sc, -float("inf"))
        return carry

    lax.fori_loop(0, n_ch, score_chunk, 0)
    tau = _select_tau_t(ST, n_ch, topk, (qpos + 1).astype(f32))

    G = N_HEADS // N_KV_HEADS
    for si, rs in enumerate(subs):
        taub = tau[:, rs]
        n_ch_s = (qb * QB + (si + 1) * AB + CH - 1) // CH
        qgs = [jnp.concatenate([q_ref[0, rs, (g * G + j) * HEAD_DIM:(g * G + j + 1) * HEAD_DIM]
                                for j in range(G)], axis=0) for g in range(N_KV_HEADS)]

        def chunk(c, carry, rs=rs, taub=taub, qgs=qgs):
            bias = jnp.where(ST[c, :, rs] >= taub, 0.0, NEG).T[None]
            off = pl.multiple_of(c * CH, CH)
            out = []
            for g in range(N_KV_HEADS):
                m, acc = carry[g]
                s = jnp.dot(qgs[g], kt_ref[0, c, g], preferred_element_type=f32)
                s = (s.reshape(G, AB, CH) + bias).reshape(G * AB, CH)
                m_new = jnp.maximum(m, jnp.max(s, axis=1, keepdims=True))
                p = jnp.exp(s - m_new).astype(bf16)
                acc = jnp.exp(m - m_new) * acc + jnp.dot(p, v_ref[0, g, pl.ds(off, CH), :],
                                                          preferred_element_type=f32)
                out.append((m_new, acc))
            return tuple(out)

        init = tuple((jnp.full((G * AB, 1), NEG, f32), jnp.zeros((G * AB, LANES), f32))
                     for _ in range(N_KV_HEADS))
        res = lax.fori_loop(0, n_ch_s, chunk, init)
        for g in range(N_KV_HEADS):
            acc = res[g][1]
            o = acc[:, :HEAD_DIM] / acc[:, HEAD_DIM:HEAD_DIM + 1]
            for j in range(G):
                h = g * G + j
                o_ref[0, rs, h * HEAD_DIM:(h + 1) * HEAD_DIM] = o[j * AB:(j + 1) * AB]


def _attn_prompt(q, k, v, qi, ki, wi):
    N, T, D = q.shape
    QB = min(Q_ROWS, T)
    topk = min(TOPK_MAX, T // 4)
    CH = min(KEY_CHUNK, T)
    NC = T // CH
    kic = ki.astype(bf16).reshape(N, NC, CH, IDX_DIM)
    qit = qi.reshape(N, T, IDX_HEADS, IDX_DIM).transpose(0, 2, 3, 1)
    wit = wi.transpose(0, 2, 1)
    kt = k.astype(bf16).reshape(N, NC, CH, N_KV_HEADS, HEAD_DIM).transpose(0, 1, 3, 4, 2)
    vh = v.astype(bf16).reshape(N, T, N_KV_HEADS, HEAD_DIM).transpose(0, 2, 1, 3)
    vh = jnp.concatenate([vh, jnp.ones(vh.shape[:3] + (1,), bf16),
                          jnp.zeros(vh.shape[:3] + (LANES - HEAD_DIM - 1,), bf16)], axis=3)
    return pl.pallas_call(
        functools.partial(_attn_prompt_kernel, topk=topk),
        grid=(N, T // QB),
        in_specs=[pl.BlockSpec((1, IDX_HEADS, IDX_DIM, QB), lambda n, b: (n, 0, 0, b)),
                  pl.BlockSpec((1, IDX_HEADS, QB), lambda n, b: (n, 0, b)),
                  pl.BlockSpec((1, NC, CH, IDX_DIM), lambda n, b: (n, 0, 0, 0)),
                  pl.BlockSpec((1, QB, D), lambda n, b: (n, b, 0)),
                  pl.BlockSpec((1, NC, N_KV_HEADS, HEAD_DIM, CH), lambda n, b: (n, 0, 0, 0, 0)),
                  pl.BlockSpec((1, N_KV_HEADS, T, LANES), lambda n, b: (n, 0, 0, 0))],
        out_specs=pl.BlockSpec((1, QB, D), lambda n, b: (n, b, 0)),
        out_shape=jax.ShapeDtypeStruct((N, T, D), f32),
        scratch_shapes=[pltpu.VMEM((NC, CH, QB), f32)],
        compiler_params=_cparams(("parallel", "arbitrary")),
        name="attn_prompt",
    )(qit, wit, kic, q, kt, vh)


def _sample_scores_kernel(pt_ref, qi_ref, w_ref, kin_ref, *rest, pp):
    pages = rest[:pp]
    sp_ref, sn_ref = rest[pp:]
    TQ = sp_ref.shape[2]
    qi = qi_ref[0]
    w = w_ref[0] * ((IDX_HEADS * IDX_DIM) ** -0.5)
    nt = (((1,), (1,)), ((), ()))

    def scores(s):
        s = jnp.maximum(s, 0.0) * w
        return jnp.sum(s.reshape(TQ, IDX_HEADS, s.shape[1]), axis=1)

    keys_t = jnp.concatenate([p[0].astype(bf16) for p in pages], axis=1)
    sp_ref[0, 0] = scores(jnp.dot(qi, keys_t, preferred_element_type=f32))

    @pl.when(pl.program_id(1) == 0)
    def _():
        s = scores(lax.dot_general(qi, kin_ref[0], nt, preferred_element_type=f32))
        t = lax.broadcasted_iota(jnp.int32, s.shape, 0)
        j = lax.broadcasted_iota(jnp.int32, s.shape, 1)
        sn_ref[0] = jnp.where(j <= t, s, -float("inf"))


def _sample_select_kernel(sp_ref, sn_ref, bias_ref, ST, *, topk, past, tq):
    NCP = sp_ref.shape[0]
    CH, RW = ST.shape[1], ST.shape[2]
    for c in range(NCP):
        ST[c] = sp_ref[c]
    ST[NCP] = jnp.concatenate([sn_ref[...], jnp.full((CH - LANES, RW), -float("inf"), f32)], axis=0)
    t = lax.broadcasted_iota(jnp.int32, (1, RW), 1) % tq
    tau = _select_tau_t(ST, NCP + 1, topk, (past + 1 + t).astype(f32))
    for c in range(NCP + 1):
        bias_ref[c] = jnp.where(ST[c] >= tau, 0.0, NEG)


def _sample_attend_kernel(pt_ref, q_ref, bp_ref, bn_ref, kn_ref, vn_ref, *rest, pp):
    kp = rest[:pp]
    vp = rest[pp:2 * pp]
    o_ref, m_ref, l_ref, acc_ref = rest[2 * pp:]
    p_id = pl.program_id(1)
    q = q_ref[0]
    R = q.shape[0]
    TQ = bp_ref.shape[2]
    nt = (((1,), (1,)), ((), ()))

    @pl.when(p_id == 0)
    def _():
        m_ref[...] = jnp.full(m_ref.shape, NEG, f32)
        l_ref[...] = jnp.zeros(l_ref.shape, f32)
        acc_ref[...] = jnp.zeros(acc_ref.shape, f32)

    def step(s, pv, bias):
        L = s.shape[1]
        b = jnp.broadcast_to(bias[:, None, :], (TQ, R // TQ, L)).reshape(R, L)
        s = jnp.where(b < 0.0, NEG, s)
        m = m_ref[...]
        m_new = jnp.maximum(m, jnp.max(s, axis=1, keepdims=True))
        p = jnp.exp(s - m_new)
        a = jnp.exp(m - m_new)
        l_ref[...] = a * l_ref[...] + jnp.sum(p, axis=1, keepdims=True)
        acc_ref[...] = a * acc_ref[...] + pv(p.astype(bf16))
        m_ref[...] = m_new

    keys_t = jnp.concatenate([p[0].astype(bf16) for p in kp], axis=1)
    vals_t = jnp.concatenate([p[0].astype(bf16) for p in vp], axis=1)
    step(jnp.dot(q, keys_t, preferred_element_type=f32),
         lambda p: lax.dot_general(p, vals_t, nt, preferred_element_type=f32), bp_ref[0, 0])

    @pl.when(p_id == pl.num_programs(1) - 1)
    def _():
        step(lax.dot_general(q, kn_ref[0], nt, preferred_element_type=f32),
             lambda p: jnp.dot(p, vn_ref[0], preferred_element_type=f32), bn_ref[0])
        o_ref[0] = acc_ref[...] / l_ref[...]


def _attn_sample(q, k_new, v_new, qi, ki_new, wi, cache_k, cache_v, cache_kidx, page_table):
    N, tq, _ = q.shape
    n_pages = page_table.shape[1]
    past = n_pages * PAGE
    topk = min(TOPK_MAX, (past + tq) // 4)
    PP = math.gcd(PAGES_PER_STEP, n_pages)
    NP = n_pages // PP
    LP = PP * PAGE
    n_pool = cache_k.shape[0]
    KV = N_KV_HEADS * HEAD_DIM

    qi2 = qi.reshape(N, tq * IDX_HEADS, IDX_DIM)
    w2 = wi.reshape(N, tq * IDX_HEADS, 1)
    pad_rows = lambda a: jnp.pad(a, ((0, 0), (0, LANES - tq), (0, 0)))
    kin = pad_rows(ki_new.astype(bf16))

    def page_spec(shape, j):
        return pl.BlockSpec(shape, lambda n, p, pt: (pt[n, p * PP + j], 0, 0))

    sp, sn = pl.pallas_call(
        functools.partial(_sample_scores_kernel, pp=PP),
        grid_spec=pltpu.PrefetchScalarGridSpec(
            num_scalar_prefetch=1, grid=(N, NP),
            in_specs=[pl.BlockSpec((1, tq * IDX_HEADS, IDX_DIM), lambda n, p, pt: (n, 0, 0)),
                      pl.BlockSpec((1, tq * IDX_HEADS, 1), lambda n, p, pt: (n, 0, 0)),
                      pl.BlockSpec((1, LANES, IDX_DIM), lambda n, p, pt: (n, 0, 0))]
                     + [page_spec((1, IDX_DIM, PAGE), j) for j in range(PP)],
            out_specs=[pl.BlockSpec((1, 1, tq, LP), lambda n, p, pt: (n, p, 0, 0)),
                       pl.BlockSpec((1, tq, LANES), lambda n, p, pt: (n, 0, 0))]),
        out_shape=[jax.ShapeDtypeStruct((N, NP, tq, LP), f32), jax.ShapeDtypeStruct((N, tq, LANES), f32)],
        compiler_params=_cparams(("parallel", "arbitrary")),
        name="sample_scores",
    )(page_table, qi2, w2, kin, *([cache_kidx.transpose(0, 2, 1)] * PP))

    CH = KEY_CHUNK
    NCP = past // CH
    RW = N * tq
    sp2 = sp.transpose(0, 2, 1, 3).reshape(RW, NCP, CH).transpose(1, 2, 0)
    bias_t = pl.pallas_call(
        functools.partial(_sample_select_kernel, topk=topk, past=past, tq=tq),
        out_shape=jax.ShapeDtypeStruct((NCP + 1, CH, RW), f32),
        scratch_shapes=[pltpu.VMEM((NCP + 1, CH, RW), f32)],
        compiler_params=pltpu.CompilerParams(vmem_limit_bytes=VMEM_LIMIT),
        name="sample_select",
    )(sp2, sn.reshape(RW, LANES).T)
    bp = bias_t[:NCP].transpose(2, 0, 1).reshape(N, tq, NP, LP).transpose(0, 2, 1, 3)
    bn = bias_t[NCP, :LANES, :].T.reshape(N, tq, LANES)

    G = N_HEADS // N_KV_HEADS
    q5 = q.reshape(N, tq, N_KV_HEADS, G, 1, HEAD_DIM)
    eye = jnp.eye(N_KV_HEADS, dtype=q.dtype).reshape(1, 1, N_KV_HEADS, 1, N_KV_HEADS, 1)
    qbd = (q5 * eye).reshape(N, tq * N_HEADS, KV)
    kn = pad_rows(k_new.astype(bf16))
    vn = pad_rows(v_new.astype(bf16))
    ck = cache_k.transpose(0, 2, 3, 1).reshape(n_pool, KV, PAGE)
    cv = cache_v.transpose(0, 2, 3, 1).reshape(n_pool, KV, PAGE)
    R = tq * N_HEADS
    o = pl.pallas_call(
        functools.partial(_sample_attend_kernel, pp=PP),
        grid_spec=pltpu.PrefetchScalarGridSpec(
            num_scalar_prefetch=1, grid=(N, NP),
            in_specs=[pl.BlockSpec((1, R, KV), lambda n, p, pt: (n, 0, 0)),
                      pl.BlockSpec((1, 1, tq, LP), lambda n, p, pt: (n, p, 0, 0)),
                      pl.BlockSpec((1, tq, LANES), lambda n, p, pt: (n, 0, 0)),
                      pl.BlockSpec((1, LANES, KV), lambda n, p, pt: (n, 0, 0)),
                      pl.BlockSpec((1, LANES, KV), lambda n, p, pt: (n, 0, 0))]
                     + [page_spec((1, KV, PAGE), j) for j in range(PP)]
                     + [page_spec((1, KV, PAGE), j) for j in range(PP)],
            out_specs=pl.BlockSpec((1, R, KV), lambda n, p, pt: (n, 0, 0)),
            scratch_shapes=[pltpu.VMEM((R, 1), f32), pltpu.VMEM((R, 1), f32), pltpu.VMEM((R, KV), f32)]),
        out_shape=jax.ShapeDtypeStruct((N, R, KV), f32),
        compiler_params=_cparams(("parallel", "arbitrary")),
        name="sample_attend",
    )(page_table, qbd, bp, bn, kn, vn, *([ck] * PP), *([cv] * PP))
    o6 = o.reshape(N, tq, N_KV_HEADS, G, N_KV_HEADS, HEAD_DIM)
    sel = jnp.eye(N_KV_HEADS, dtype=f32).reshape(1, 1, N_KV_HEADS, 1, N_KV_HEADS, 1)
    return jnp.sum(o6 * sel, axis=4).reshape(N, tq, N_HEADS * HEAD_DIM)


HALO = 32


def _conv_kernel(u_ref, prev_ref, cw_ref, cb_ref, g_ref, b_ref, wo_ref, o_ref, buf):
    tt = u_ref.shape[1]

    @pl.when(pl.program_id(1) == 0)
    def _():
        buf[0:HALO] = prev_ref[0]

    @pl.when(pl.program_id(1) > 0)
    def _():
        buf[0:HALO] = buf[tt:tt + HALO]

    buf[HALO:HALO + tt] = u_ref[0]
    off = HALO - (D_CONV_W - 1)
    y = buf[off:off + tt] * cw_ref[0:1, :]
    for j in range(1, D_CONV_W):
        y = y + buf[off + j:off + j + tt] * cw_ref[j:j + 1, :]
    y = y + cb_ref[...]
    mu = jnp.mean(y, axis=-1, keepdims=True)
    yc = y - mu
    var = jnp.mean(yc * yc, axis=-1, keepdims=True)
    y = yc * lax.rsqrt(var + LN_EPS) * g_ref[...] + b_ref[...]
    y = y * jax.nn.sigmoid(y)
    o_ref[0] = jnp.dot(y.astype(bf16), wo_ref[...], preferred_element_type=f32)


def _conv_module(u, prev, conv_w, conv_b, ln_g, ln_b, w_conv_out_bf, tt):
    N, T, C = u.shape
    D = w_conv_out_bf.shape[1]
    prev_pad = jnp.pad(prev, ((0, 0), (HALO - prev.shape[1], 0), (0, 0)))
    cw = jnp.pad(conv_w, ((0, 32 - conv_w.shape[0]), (0, 0)))
    row2 = lambda a: a.reshape(1, -1)
    return pl.pallas_call(
        _conv_kernel,
        grid=(N, T // tt),
        in_specs=[pl.BlockSpec((1, tt, C), lambda n, t: (n, t, 0)),
                  pl.BlockSpec((1, HALO, C), lambda n, t: (n, 0, 0)),
                  pl.BlockSpec((32, C), lambda n, t: (0, 0)),
                  pl.BlockSpec((1, C), lambda n, t: (0, 0)),
                  pl.BlockSpec((1, C), lambda n, t: (0, 0)),
                  pl.BlockSpec((1, C), lambda n, t: (0, 0)),
                  pl.BlockSpec((C, D), lambda n, t: (0, 0))],
        out_specs=pl.BlockSpec((1, tt, D), lambda n, t: (n, t, 0)),
        out_shape=jax.ShapeDtypeStruct((N, T, D), f32),
        scratch_shapes=[pltpu.VMEM((HALO + tt + 8, C), f32)],
        compiler_params=_cparams(("parallel", "arbitrary")),
        name="conv_module",
    )(u, prev_pad, cw, row2(conv_b), row2(ln_g), row2(ln_b), w_conv_out_bf)


def _merge_kernel(x_ref, co_ref, ao_ref, gc_ref, ga_ref, wo_ref, g_ref, b_ref, ht_ref, htb_ref):
    m = gc_ref[...] * co_ref[...] + ga_ref[...] * ao_ref[...]
    z = ALPHA * x_ref[...] + jnp.dot(m.astype(bf16), wo_ref[...], preferred_element_type=f32)
    mu = jnp.mean(z, axis=-1, keepdims=True)
    zc = z - mu
    var = jnp.mean(zc * zc, axis=-1, keepdims=True)
    ht = (zc * lax.rsqrt(var + LN_EPS) * g_ref[...] + b_ref[...]).T
    ht_ref[...] = ht
    htb_ref[...] = ht.astype(bf16)


def _merge(x2, conv_o, attn_o, sgc, sga, w_out_bf, ln_g, ln_b, tm):
    R, D = x2.shape
    row = lambda i: (i, 0)
    cst = lambda i: (0, 0)
    col = lambda i: (0, i)
    return pl.pallas_call(
        _merge_kernel,
        grid=(R // tm,),
        in_specs=[pl.BlockSpec((tm, D), row)] * 5
                 + [pl.BlockSpec((D, D), cst), pl.BlockSpec((1, D), cst), pl.BlockSpec((1, D), cst)],
        out_specs=[pl.BlockSpec((D, tm), col), pl.BlockSpec((D, tm), col)],
        out_shape=[jax.ShapeDtypeStruct((D, R), f32), jax.ShapeDtypeStruct((D, R), bf16)],
        compiler_params=_cparams(("parallel",)),
        name="merge",
    )(x2, conv_o, attn_o, sgc, sga, w_out_bf, ln_g.reshape(1, D), ln_b.reshape(1, D))


def _extract16(s, idx, exact):
    rank = jnp.full(s.shape, float(PEER_TOPK), f32)
    tops = []
    for r in range(PEER_TOPK):
        m = jnp.max(s, axis=0, keepdims=True)
        if exact:
            pick = idx == jnp.min(jnp.where(s == m, idx, 1e9), axis=0, keepdims=True)
        else:
            pick = s == m
        rank = jnp.where(pick, float(r), rank)
        s = jnp.where(pick, -float("inf"), s)
        tops.append(m)
    return rank, jnp.concatenate(tops, axis=0)


def _route_head(s1, s2, exact):
    K, Tn = s1.shape
    T16 = float(PEER_TOPK)
    ninf = -float("inf")
    rows = lax.broadcasted_iota(jnp.int32, (K, Tn), 0).astype(f32)
    sub = lax.broadcasted_iota(jnp.int32, (8, Tn), 0)
    subf = sub.astype(f32)
    r1, t1 = _extract16(s1, rows, exact)
    r2, t2 = _extract16(s2, rows, exact)
    limits = [16, 8, 5, 4, 3, 2, 2, 2]
    cands, idxs = [], []
    cands.append(t1[0:1] + t2[0:8]); idxs.append(subf)
    cands.append(t1[0:1] + t2[8:16]); idxs.append(subf + 8.0)
    for a in range(1, 8):
        cands.append(jnp.where(sub < limits[a], t1[a:a + 1] + t2[0:8], ninf))
        idxs.append(subf + float(16 * a))
    cands.append(t1[8:16] + t2[0:1]); idxs.append((subf + 8.0) * 16.0)
    cand = jnp.concatenate(cands, axis=0)
    cidx = jnp.concatenate(idxs, axis=0)
    ecand = jnp.exp(cand - cand[0:1])
    rc, _ = _extract16(cand, cidx, exact)
    picked = jnp.where(rc < T16, 1.0, 0.0)
    z = jnp.sum(picked * ecand, axis=0, keepdims=True)
    la = [jnp.sum(picked[0:16], axis=0, keepdims=True)]
    for a in range(1, 8):
        la.append(jnp.sum(picked[8 + 8 * a:16 + 8 * a], axis=0, keepdims=True))
    ltail = picked[72:80]
    c = jnp.zeros(s1.shape, f32)
    for a in range(8):
        c = jnp.where(r1 == float(a), la[a], c)
    for a in range(8, 16):
        c = jnp.where(r1 == float(a), ltail[a - 8:a - 7], c)
    in1 = r1 < T16
    in2 = r2 < T16
    e1 = jnp.where(in1, jnp.exp(s1 - t1[0:1]), 0.0) / z
    e2 = jnp.where(in2, jnp.exp(s2 - t2[0:1]), 0.0)
    n1 = jnp.sum(jnp.where(in1, 1.0, 0.0), axis=0, keepdims=True)
    n2 = jnp.sum(jnp.where(in2, 1.0, 0.0), axis=0, keepdims=True)
    nc = jnp.sum(picked, axis=0, keepdims=True)
    ok = (n1 == T16) & (n2 == T16) & (nc == T16)
    return r2, e2, c, e1, ok


def _route_kernel(ht_ref, wq_ref, k1_ref, k2_ref, r2_ref, e2_ref, c_ref, e1_ref):
    qh = jnp.dot(wq_ref[...], ht_ref[...], preferred_element_type=f32)
    half = PEER_DQ // 2

    def store(h, r2, e2, c, e1):
        r2_ref[h] = r2.astype(bf16)
        e2_ref[h] = e2.astype(bf16)
        c_ref[h] = c
        e1_ref[h] = e1

    for h in range(PEER_HEADS):
        q1 = qh[h * PEER_DQ:h * PEER_DQ + half].astype(bf16)
        q2 = qh[h * PEER_DQ + half:(h + 1) * PEER_DQ].astype(bf16)
        s1 = jnp.dot(k1_ref[h], q1, preferred_element_type=f32)
        s2 = jnp.dot(k2_ref[h], q2, preferred_element_type=f32)
        r2, e2, c, e1, ok = _route_head(s1, s2, exact=False)
        store(h, r2, e2, c, e1)

        @pl.when(jnp.min(jnp.where(ok, 1.0, 0.0)) < 0.5)
        def _(h=h, s1=s1, s2=s2):
            store(h, *_route_head(s1, s2, exact=True)[:4])


def _route(ht_bf, wqt_bf, k1_bf, k2_bf, tn):
    D, R = ht_bf.shape
    shp = lambda dt: jax.ShapeDtypeStruct((PEER_HEADS, PEER_NKEYS, R), dt)
    ospec = pl.BlockSpec((PEER_HEADS, PEER_NKEYS, tn), lambda i: (0, 0, i))
    return pl.pallas_call(
        _route_kernel,
        grid=(R // tn,),
        in_specs=[pl.BlockSpec((D, tn), lambda i: (0, i)),
                  pl.BlockSpec(wqt_bf.shape, lambda i: (0, 0)),
                  pl.BlockSpec(k1_bf.shape, lambda i: (0, 0, 0)),
                  pl.BlockSpec(k2_bf.shape, lambda i: (0, 0, 0))],
        out_specs=[ospec] * 4,
        out_shape=[shp(bf16), shp(bf16), shp(f32), shp(f32)],
        compiler_params=_cparams(("parallel",)),
        name="peer_route",
    )(ht_bf, wqt_bf, k1_bf, k2_bf)


ET = 2048
PEER_SUBTILE = 256


def _peer_kernel(htb_ref, ht_ref, u_ref, vt_ref, r2_ref, e2_ref, c_ref, e1_ref, g_ref, b_ref,
                 y_ref, acc_ref, p_ref):
    et = pl.program_id(1)
    NK = PEER_NKEYS

    @pl.when(et == 0)
    def _():
        acc_ref[...] = jnp.zeros(acc_ref.shape, f32)

    Tn = htb_ref.shape[1]
    SL = 16
    TS = p_ref.shape[2]
    zero = jnp.zeros((), bf16)
    acts = [[jnp.dot(u_ref[ii * NK:(ii + 1) * NK, :], htb_ref[:, ts * TS:(ts + 1) * TS],
                     preferred_element_type=f32) for ii in range(ET // NK)] for ts in range(Tn // TS)]
    for ts in range(Tn // TS):
        tok = slice(ts * TS, (ts + 1) * TS)
        for ii in range(ET // NK):
            a = acts[ts][ii]
            gl = (0.5 * a * (1.0 + lax.erf(a * (2.0 ** -0.5)))).astype(bf16)
            w = [jnp.zeros((SL, TS), bf16) for _ in range(NK // SL)]
            for h in range(PEER_HEADS):
                cb = jnp.broadcast_to(c_ref[h, ii:ii + 1, tok], (SL, TS)).astype(bf16)
                eb = jnp.broadcast_to(e1_ref[h, ii:ii + 1, tok], (SL, TS)).astype(bf16)
                for s in range(NK // SL):
                    rows = slice(s * SL, (s + 1) * SL)
                    w[s] = w[s] + jnp.where(r2_ref[h, rows, tok] < cb, eb * e2_ref[h, rows, tok], zero)
            for s in range(NK // SL):
                p_ref[ts, ii * NK + s * SL:ii * NK + (s + 1) * SL, :] = w[s] * gl[s * SL:(s + 1) * SL]
        acc_ref[:, tok] += jnp.dot(vt_ref[...], p_ref[ts], preferred_element_type=f32)

    @pl.when(et == pl.num_programs(1) - 1)
    def _():
        z = ALPHA * ht_ref[...] + acc_ref[...]
        mu = jnp.mean(z, axis=0, keepdims=True)
        zc = z - mu
        var = jnp.mean(zc * zc, axis=0, keepdims=True)
        y_ref[...] = (zc * lax.rsqrt(var + LN_EPS) * g_ref[...] + b_ref[...]).T


def _peer_dense(ht_bf, ht, u_bf, vt_bf, r2, e2, c, e1, ln_g, ln_b, tn):
    D, R = ht.shape
    E = u_bf.shape[0]
    NI = ET // PEER_NKEYS
    ts = min(tn, PEER_SUBTILE)
    tok = lambda i, e: (0, i)
    rt = pl.BlockSpec((PEER_HEADS, PEER_NKEYS, tn), lambda i, e: (0, 0, i))
    ri = pl.BlockSpec((PEER_HEADS, NI, tn), lambda i, e: (0, e, i))
    return pl.pallas_call(
        _peer_kernel,
        grid=(R // tn, E // ET),
        in_specs=[pl.BlockSpec((D, tn), tok), pl.BlockSpec((D, tn), tok),
                  pl.BlockSpec((ET, D), lambda i, e: (e, 0)),
                  pl.BlockSpec((D, ET), lambda i, e: (0, e)),
                  rt, rt, ri, ri,
                  pl.BlockSpec((D, 1), lambda i, e: (0, 0)), pl.BlockSpec((D, 1), lambda i, e: (0, 0))],
        out_specs=pl.BlockSpec((tn, D), lambda i, e: (i, 0)),
        out_shape=jax.ShapeDtypeStruct((R, D), f32),
        scratch_shapes=[pltpu.VMEM((D, tn), f32), pltpu.VMEM((tn // ts, ET, ts), bf16)],
        compiler_params=_cparams(("parallel", "arbitrary")),
        name="peer_dense",
    )(ht_bf, ht, u_bf, vt_bf, r2, e2, c, e1, ln_g.reshape(D, 1), ln_b.reshape(D, 1))


def _channel_mix(ht, ht_bf, wqt_bf, k1_bf, k2_bf, u_bf, vt_bf, ln_g, ln_b, tn_route, tn_dense):
    r2, e2, c, e1 = _route(ht_bf, wqt_bf, k1_bf, k2_bf, tn_route)
    return _peer_dense(ht_bf, ht, u_bf, vt_bf, r2, e2, c, e1, ln_g, ln_b, tn_dense)


def _prep_weights(w_in, w_conv_out, w_out, peer_w_query, peer_keys1, peer_keys2, peer_u, peer_v):
    D = w_in.shape[0]
    split = 2 * (D // 2) + N_HEADS * HEAD_DIM + 2 * N_KV_HEADS * HEAD_DIM + IDX_HEADS * IDX_DIM + IDX_DIM + IDX_HEADS
    pad = (-split) % LANES
    w_pad = jnp.concatenate([w_in[:, :split], jnp.zeros((D, pad), w_in.dtype), w_in[:, split:]], axis=1).astype(bf16)
    wqt = peer_w_query.reshape(D, PEER_HEADS * PEER_DQ).T.astype(bf16)
    return dict(w_pad=w_pad, wco=w_conv_out.astype(bf16), wo=w_out.astype(bf16), wqt=wqt,
                k1=peer_keys1.astype(bf16), k2=peer_keys2.astype(bf16),
                u=peer_u.astype(bf16), vt=peer_v.T.astype(bf16))


def _group(x, pos, prev, W, conv_w, conv_b, conv_ln_g, conv_ln_b, ln1_g, ln1_b, ln2_g, ln2_b,
           attn_fn, tm, tt, tn_route, tn_dense):
    N, T, D = x.shape
    R = N * T
    x2 = x.reshape(R, D)
    tabs = _rope_tables(pos if T % tm == 0 else jnp.tile(pos, tm // T))
    u, q, k, v, qi, ki, wi, sgc, sga = _project(x2, W["w_pad"], tabs, tm)
    r3 = lambda a: a.reshape(N, T, a.shape[-1])
    attn_o = attn_fn(r3(q), r3(k), r3(v), r3(qi), r3(ki), r3(wi))
    u3 = r3(u)
    if T % tt == 0:
        conv_o = _conv_module(u3, prev, conv_w, conv_b, conv_ln_g, conv_ln_b, W["wco"], tt)
    else:
        up = jnp.pad(u3, ((0, 0), (0, tt - T), (0, 0)))
        conv_o = _conv_module(up, prev, conv_w, conv_b, conv_ln_g, conv_ln_b, W["wco"], tt)[:, :T]
    ht, ht_bf = _merge(x2, conv_o.reshape(R, D), attn_o.reshape(R, D), sgc, sga, W["wo"], ln1_g, ln1_b, tm)
    y2 = _channel_mix(ht, ht_bf, W["wqt"], W["k1"], W["k2"], W["u"], W["vt"], ln2_g, ln2_b, tn_route, tn_dense)
    return y2.reshape(N, T, D), k, v, ki, u3


def kernel(x_prompt, x_sample, cache_k, cache_v, cache_kidx, state_conv, page_table, w_in, conv_w, conv_b,
           conv_ln_g, conv_ln_b, w_conv_out, w_out, ln1_g, ln1_b, peer_w_query, peer_keys1, peer_keys2,
           peer_u, peer_v, ln2_g, ln2_b):
    W = _prep_weights(w_in, w_conv_out, w_out, peer_w_query, peer_keys1, peer_keys2, peer_u, peer_v)
    common = (W, conv_w, conv_b, conv_ln_g, conv_ln_b, ln1_g, ln1_b, ln2_g, ln2_b)
    C = conv_w.shape[1]
    keep = D_CONV_W - 1

    N, T, D = x_prompt.shape
    tm = min(256, N * T)
    yp, k_p, v_p, kidx_p, u_p = _group(
        x_prompt, jnp.arange(T, dtype=jnp.int32), jnp.zeros((N, keep, C), f32), *common,
        _attn_prompt, tm, min(512, T), min(512, N * T), min(512, N * T))
    conv_p = u_p[:, T - keep:]

    NS, tq, _ = x_sample.shape
    past = page_table.shape[1] * PAGE
    attn_s = functools.partial(_attn_sample, cache_k=cache_k, cache_v=cache_v, cache_kidx=cache_kidx,
                               page_table=page_table)
    rs = NS * tq
    ys, k_s, v_s, kidx_s, u_s = _group(
        x_sample, past + jnp.arange(tq, dtype=jnp.int32), state_conv, *common,
        lambda q, k, v, qi, ki, wi: attn_s(q, k, v, qi, ki, wi), rs, 8, rs, rs)
    conv_s = jnp.concatenate([state_conv, u_s], axis=1)[:, -keep:]

    kv4 = lambda a, n, t: a.reshape(n, t, N_KV_HEADS, HEAD_DIM)
    return (yp, ys, kv4(k_p, N, T), kv4(v_p, N, T), kidx_p.reshape(N, T, IDX_DIM), conv_p,
            kv4(k_s, NS, tq), kv4(v_s, NS, tq), kidx_s.reshape(NS, tq, IDX_DIM), conv_s)
```

```python
import functools
import math

import jax
import jax.numpy as jnp
from jax import lax
from jax.experimental import pallas as pl
from jax.experimental.pallas import tpu as pltpu

f32 = jnp.float32
bf16 = jnp.bfloat16

D_CONV_W = 31
N_HEADS = 16
N_KV_HEADS = 4
HEAD_DIM = 64
IDX_HEADS = 8
IDX_DIM = 64
TOPK_MAX = 256
ROPE_THETA = 500000.0
ROPE_HALF = 8
PAGE = 128
PEER_HEADS = 8
PEER_NKEYS = 128
PEER_DQ = 128
PEER_TOPK = 16
ALPHA = 2.0 ** 0.25
LN_EPS = 1e-5

LANES = 128
VMEM_LIMIT = 56 * 1024 * 1024
NEG = -1e30
KEY_CHUNK = 512
Q_ROWS = 256
ATTN_ROWS = 256
PAGES_PER_STEP = 16


def _cparams(sem):
    return pltpu.CompilerParams(dimension_semantics=sem, vmem_limit_bytes=VMEM_LIMIT)


def _rope_tables(pos):
    inv = jnp.power(ROPE_THETA, -jnp.arange(ROPE_HALF, dtype=f32) / ROPE_HALF)
    ang = pos.astype(f32)[:, None] * inv
    cos, sin = jnp.cos(ang), jnp.sin(ang)
    d = jnp.arange(HEAD_DIM)
    cosp = jnp.where(d < 2 * ROPE_HALF, cos[:, d % ROPE_HALF], 1.0)
    sap = jnp.where(d < ROPE_HALF, -sin[:, d % ROPE_HALF], 0.0)
    sbp = jnp.where((d >= ROPE_HALF) & (d < 2 * ROPE_HALF), sin[:, d % ROPE_HALF], 0.0)
    one = jnp.ones_like(cosp)
    zero = jnp.zeros_like(cosp)
    return (jnp.concatenate([cosp, cosp, cosp, one], axis=1),
            jnp.concatenate([sap, sap, sap, zero], axis=1),
            jnp.concatenate([sbp, sbp, sbp, zero], axis=1))


def _rope(z, cos, sa, sb):
    outs = []
    for c in range(z.shape[1] // LANES):
        zc = z[:, c * LANES:(c + 1) * LANES]
        outs.append(zc * cos + pltpu.roll(zc, LANES - ROPE_HALF, 1) * sa + pltpu.roll(zc, ROPE_HALF, 1) * sb)
    return outs[0] if len(outs) == 1 else jnp.concatenate(outs, axis=1)


def _proj_kernel(x_ref, w_ref, cos_ref, sa_ref, sb_ref,
                 u_ref, q_ref, k_ref, v_ref, qi_ref, ki_ref, wi_ref, gc_ref, ga_ref):
    x = x_ref[...].astype(bf16)
    cos, sa, sb = cos_ref[:, :LANES], sa_ref[:, :LANES], sb_ref[:, :LANES]
    cos2, sa2, sb2 = cos_ref[:, LANES:], sa_ref[:, LANES:], sb_ref[:, LANES:]

    def mm(c0, c1):
        return jnp.dot(x, w_ref[:, c0:c1], preferred_element_type=f32)

    z = mm(0, 1024)
    u_ref[...] = z[:, :512] * jax.nn.sigmoid(z[:, 512:])
    z = mm(1024, 2048)
    q_ref[...] = (_rope(z, cos, sa, sb) * (HEAD_DIM ** -0.5)).astype(bf16)
    z = mm(2048, 2560)
    k_ref[...] = _rope(z[:, :256], cos, sa, sb)
    v_ref[...] = z[:, 256:]
    z = mm(2560, 3072)
    qi_ref[...] = _rope(z, cos, sa, sb).astype(bf16)
    z = mm(3072, 3200)
    z = _rope(z, cos2, sa2, sb2)
    ki_ref[...] = z[:, :IDX_DIM]
    wi_ref[...] = z[:, IDX_DIM:IDX_DIM + IDX_HEADS]
    gc_ref[...] = jax.nn.sigmoid(mm(3200, 4224))
    ga_ref[...] = jax.nn.sigmoid(mm(4224, 5248))


def _project(x2, w_pad, tabs, tm):
    R, D = x2.shape
    rt = tabs[0].shape[0]
    nt = rt // tm
    row = lambda i: (i, 0)
    tab = lambda i: (i % nt, 0)
    widths = [(512, f32), (1024, bf16), (256, f32), (256, f32), (512, bf16), (IDX_DIM, f32), (IDX_HEADS, f32),
              (1024, f32), (1024, f32)]
    return pl.pallas_call(
        _proj_kernel,
        grid=(R // tm,),
        in_specs=[pl.BlockSpec((tm, D), row),
                  pl.BlockSpec(w_pad.shape, lambda i: (0, 0)),
                  pl.BlockSpec((tm, 256), tab), pl.BlockSpec((tm, 256), tab), pl.BlockSpec((tm, 256), tab)],
        out_specs=[pl.BlockSpec((tm, w), row) for w, _ in widths],
        out_shape=[jax.ShapeDtypeStruct((R, w), dt) for w, dt in widths],
        compiler_params=_cparams(("parallel",)),
        name="proj",
    )(x2, w_pad, *tabs)


def _select_tau_t(ST, n_ch, k, n_adm):
    _, CH, RW = ST.shape
    kf = float(k)
    inf = float("inf")
    SUB = 8

    def fold(x):
        return x.reshape(CH // SUB, SUB, RW)

    def mm_body(c, carry):
        mn, mx = carry
        s = ST[c]
        mx = jnp.maximum(mx, jnp.max(fold(s), axis=0))
        mn = jnp.minimum(mn, jnp.min(fold(jnp.where(s == -inf, inf, s)), axis=0))
        return mn, mx

    mn, mx = lax.fori_loop(0, n_ch, mm_body, (jnp.full((SUB, RW), inf, f32), jnp.full((SUB, RW), -inf, f32)))
    mn = jnp.min(mn, axis=0, keepdims=True)
    mx = jnp.max(mx, axis=0, keepdims=True)
    hi0 = mx + jnp.maximum(jnp.abs(mx), 1e-30) * (2.0 ** -20)

    def count_gt(t):
        def body(c, acc):
            return acc + jnp.sum(fold(jnp.where(ST[c] > t, 1.0, 0.0)), axis=0)

        return jnp.sum(lax.fori_loop(0, n_ch, body, jnp.zeros((SUB, RW), f32)), axis=0, keepdims=True)

    def probe(t):
        def body(c, carry):
            acc, bm = carry
            s = ST[c]
            hit = s >= t
            acc = acc + jnp.sum(fold(jnp.where(hit, 1.0, 0.0)), axis=0)
            bm = jnp.maximum(bm, jnp.max(fold(jnp.where(hit, -inf, s)), axis=0))
            return acc, bm

        acc, bm = lax.fori_loop(0, n_ch, body, (jnp.zeros((SUB, RW), f32), jnp.full((SUB, RW), -inf, f32)))
        return jnp.sum(acc, axis=0, keepdims=True), jnp.max(bm, axis=0, keepdims=True)

    def to_key(x):
        b = lax.bitcast_convert_type(x, jnp.int32)
        return b ^ ((b >> 31) & 0x7FFFFFFF)

    all_sel = n_adm <= kf
    RUN, STALL, DONE, TIED = 0.0, 1.0, 2.0, 3.0

    def cond(st):
        return jnp.logical_and(jnp.min(st[5]) < DONE, st[6] < 400)

    def body(st):
        lo, hi, hb, cl, ch, flag, it = st

        def value_half():
            h = lo + (hi - lo) * 0.5
            return h, jnp.where(h <= lo, 1.0, jnp.where(h >= hi, 1.0, 0.0))

        def key_half():
            lk, hk = to_key(lo), to_key(hi)
            midk = (lk >> 1) + (hk >> 1) + (lk & hk & 1)
            h = lax.bitcast_convert_type(midk ^ ((midk >> 31) & 0x7FFFFFFF), f32)
            return h, jnp.where(midk == lk, 1.0, 0.0)

        half, adj = lax.cond(it >= 24, key_half, value_half)
        descf = jnp.where(cl - ch <= 4.0, 1.0, jnp.where(flag == STALL, 1.0, adj))
        mid = jnp.where(descf > 0.5, hb, half)
        cnt, b = probe(mid)
        ge = cnt >= kf
        fin = jnp.where(cnt == kf, 1.0, jnp.where(ge, descf, 0.0))
        idle = jnp.where(cnt == cl, STALL, jnp.where(cnt == ch, STALL, RUN))
        nxt = jnp.where(fin > 0.5, jnp.where(cnt > kf, TIED, DONE), idle)
        running = flag < DONE
        return (jnp.where(running, jnp.where(ge, mid, lo), lo), jnp.where(ge, hi, mid), jnp.where(ge, hb, b),
                jnp.where(ge, cnt, cl), jnp.where(ge, ch, cnt), jnp.where(running, nxt, flag), it + 1)

    st = lax.while_loop(cond, body, (jnp.where(all_sel, -3e38, mn), hi0, mx, n_adm, jnp.zeros((1, RW), f32),
                                     jnp.where(all_sel, DONE, RUN), jnp.int32(0)))
    tau = st[0]
    tie = st[5] == TIED

    @pl.when(jnp.max(st[5]) > DONE)
    def _():
        need = kf - count_gt(tau)
        r = lax.broadcasted_iota(jnp.int32, (CH, CH), 0)
        c_ = lax.broadcasted_iota(jnp.int32, (CH, CH), 1)
        tri = jnp.where(c_ <= r, 1.0, 0.0).astype(bf16)

        def fix(c, run):
            s = ST[c]
            eqf = jnp.where(tie, jnp.where(s == tau, 1.0, 0.0), 0.0)
            pref = jnp.dot(tri, eqf.astype(bf16), preferred_element_type=f32) + run
            ST[c] = jnp.where(eqf * pref > need, -inf, s)
            return run + jnp.sum(eqf, axis=0, keepdims=True)

        lax.fori_loop(0, n_ch, fix, jnp.zeros((1, RW), f32))

    return tau


def _attn_prompt_kernel(qit_ref, wit_ref, ki_ref, q_ref, kt_ref, v_ref, o_ref, ST, *, topk):
    qb = pl.program_id(1)
    QB = q_ref.shape[1]
    AB = min(QB, ATTN_ROWS)
    CH = ST.shape[1]
    n_ch = (qb * QB + QB + CH - 1) // CH
    qpos = qb * QB + lax.broadcasted_iota(jnp.int32, (1, QB), 1)
    wsc = wit_ref[0] * ((IDX_HEADS * IDX_DIM) ** -0.5)
    subs = [slice(r0, r0 + AB) for r0 in range(0, QB, AB)]
    qit = jnp.concatenate([qit_ref[0, h] for h in range(IDX_HEADS)], axis=1)

    def score_chunk(c, carry):
        s = jnp.dot(ki_ref[0, c], qit, preferred_element_type=f32)
        sc = jnp.maximum(s[:, 0:QB], 0.0) * wsc[0:1]
        for h in range(1, IDX_HEADS):
            sc = sc + jnp.maximum(s[:, h * QB:(h + 1) * QB], 0.0) * wsc[h:h + 1]
        kpos = c * CH + lax.broadcasted_iota(jnp.int32, (CH, 1), 0)
        ST[c] = jnp.where(kpos <= qpos, sc, -float("inf"))
        return carry

    lax.fori_loop(0, n_ch, score_chunk, 0)
    tau = _select_tau_t(ST, n_ch, topk, (qpos + 1).astype(f32))

    G = N_HEADS // N_KV_HEADS
    for si, rs in enumerate(subs):
        taub = tau[:, rs]
        n_ch_s = (qb * QB + (si + 1) * AB + CH - 1) // CH
        qgs = [jnp.concatenate([q_ref[0, rs, (g * G + j) * HEAD_DIM:(g * G + j + 1) * HEAD_DIM]
                                for j in range(G)], axis=0) for g in range(N_KV_HEADS)]

        def chunk(c, carry, rs=rs, taub=taub, qgs=qgs):
            bias = jnp.where(ST[c, :, rs] >= taub, 0.0, NEG).T[None]
            off = pl.multiple_of(c * CH, CH)
            out = []
            for g in range(N_KV_HEADS):
                m, acc = carry[g]
                s = jnp.dot(qgs[g], kt_ref[0, c, g], preferred_element_type=f32)
                s = (s.reshape(G, AB, CH) + bias).reshape(G * AB, CH)
                m_new = jnp.maximum(m, jnp.max(s, axis=1, keepdims=True))
                p = jnp.exp(s - m_new).astype(bf16)
                acc = jnp.exp(m - m_new) * acc + jnp.dot(p, v_ref[0, g, pl.ds(off, CH), :],
                                                          preferred_element_type=f32)
                out.append((m_new, acc))
            return tuple(out)

        init = tuple((jnp.full((G * AB, 1), NEG, f32), jnp.zeros((G * AB, LANES), f32))
                     for _ in range(N_KV_HEADS))
        res = lax.fori_loop(0, n_ch_s, chunk, init)
        for g in range(N_KV_HEADS):
            acc = res[g][1]
            o = acc[:, :HEAD_DIM] / acc[:, HEAD_DIM:HEAD_DIM + 1]
            for j in range(G):
                h = g * G + j
                o_ref[0, rs, h * HEAD_DIM:(h + 1) * HEAD_DIM] = o[j * AB:(j + 1) * AB]


def _attn_prompt(q, k, v, qi, ki, wi):
    N, T, D = q.shape
    QB = min(Q_ROWS, T)
    topk = min(TOPK_MAX, T // 4)
    CH = min(KEY_CHUNK, T)
    NC = T // CH
    kic = ki.astype(bf16).reshape(N, NC, CH, IDX_DIM)
    qit = qi.reshape(N, T, IDX_HEADS, IDX_DIM).transpose(0, 2, 3, 1)
    wit = wi.transpose(0, 2, 1)
    kt = k.astype(bf16).reshape(N, NC, CH, N_KV_HEADS, HEAD_DIM).transpose(0, 1, 3, 4, 2)
    vh = v.astype(bf16).reshape(N, T, N_KV_HEADS, HEAD_DIM).transpose(0, 2, 1, 3)
    vh = jnp.concatenate([vh, jnp.ones(vh.shape[:3] + (1,), bf16),
                          jnp.zeros(vh.shape[:3] + (LANES - HEAD_DIM - 1,), bf16)], axis=3)
    return pl.pallas_call(
        functools.partial(_attn_prompt_kernel, topk=topk),
        grid=(N, T // QB),
        in_specs=[pl.BlockSpec((1, IDX_HEADS, IDX_DIM, QB), lambda n, b: (n, 0, 0, b)),
                  pl.BlockSpec((1, IDX_HEADS, QB), lambda n, b: (n, 0, b)),
                  pl.BlockSpec((1, NC, CH, IDX_DIM), lambda n, b: (n, 0, 0, 0)),
                  pl.BlockSpec((1, QB, D), lambda n, b: (n, b, 0)),
                  pl.BlockSpec((1, NC, N_KV_HEADS, HEAD_DIM, CH), lambda n, b: (n, 0, 0, 0, 0)),
                  pl.BlockSpec((1, N_KV_HEADS, T, LANES), lambda n, b: (n, 0, 0, 0))],
        out_specs=pl.BlockSpec((1, QB, D), lambda n, b: (n, b, 0)),
        out_shape=jax.ShapeDtypeStruct((N, T, D), f32),
        scratch_shapes=[pltpu.VMEM((NC, CH, QB), f32)],
        compiler_params=_cparams(("parallel", "arbitrary")),
        name="attn_prompt",
    )(qit, wit, kic, q, kt, vh)


def _sample_scores_kernel(pt_ref, qi_ref, w_ref, kin_ref, *rest, pp):
    pages = rest[:pp]
    sp_ref, sn_ref = rest[pp:]
    TQ = sp_ref.shape[2]
    qi = qi_ref[0]
    w = w_ref[0] * ((IDX_HEADS * IDX_DIM) ** -0.5)
    nt = (((1,), (1,)), ((), ()))

    def scores(s):
        s = jnp.maximum(s, 0.0) * w
        return jnp.sum(s.reshape(TQ, IDX_HEADS, s.shape[1]), axis=1)

    keys_t = jnp.concatenate([p[0].astype(bf16) for p in pages], axis=1)
    sp_ref[0, 0] = scores(jnp.dot(qi, keys_t, preferred_element_type=f32))

    @pl.when(pl.program_id(1) == 0)
    def _():
        s = scores(lax.dot_general(qi, kin_ref[0], nt, preferred_element_type=f32))
        t = lax.broadcasted_iota(jnp.int32, s.shape, 0)
        j = lax.broadcasted_iota(jnp.int32, s.shape, 1)
        sn_ref[0] = jnp.where(j <= t, s, -float("inf"))


def _sample_select_kernel(sp_ref, sn_ref, bias_ref, ST, *, topk, past, tq):
    NCP = sp_ref.shape[0]
    CH, RW = ST.shape[1], ST.shape[2]
    for c in range(NCP):
        ST[c] = sp_ref[c]
    ST[NCP] = jnp.concatenate([sn_ref[...], jnp.full((CH - LANES, RW), -float("inf"), f32)], axis=0)
    t = lax.broadcasted_iota(jnp.int32, (1, RW), 1) % tq
    tau = _select_tau_t(ST, NCP + 1, topk, (past + 1 + t).astype(f32))
    for c in range(NCP + 1):
        bias_ref[c] = jnp.where(ST[c] >= tau, 0.0, NEG)


def _sample_attend_kernel(pt_ref, q_ref, bp_ref, bn_ref, kn_ref, vn_ref, *rest, pp):
    kp = rest[:pp]
    vp = rest[pp:2 * pp]
    o_ref, m_ref, l_ref, acc_ref = rest[2 * pp:]
    p_id = pl.program_id(1)
    q = q_ref[0]
    R = q.shape[0]
    TQ = bp_ref.shape[2]
    nt = (((1,), (1,)), ((), ()))

    @pl.when(p_id == 0)
    def _():
        m_ref[...] = jnp.full(m_ref.shape, NEG, f32)
        l_ref[...] = jnp.zeros(l_ref.shape, f32)
        acc_ref[...] = jnp.zeros(acc_ref.shape, f32)

    def step(s, pv, bias):
        L = s.shape[1]
        b = jnp.broadcast_to(bias[:, None, :], (TQ, R // TQ, L)).reshape(R, L)
        s = jnp.where(b < 0.0, NEG, s)
        m = m_ref[...]
        m_new = jnp.maximum(m, jnp.max(s, axis=1, keepdims=True))
        p = jnp.exp(s - m_new)
        a = jnp.exp(m - m_new)
        l_ref[...] = a * l_ref[...] + jnp.sum(p, axis=1, keepdims=True)
        acc_ref[...] = a * acc_ref[...] + pv(p.astype(bf16))
        m_ref[...] = m_new

    keys_t = jnp.concatenate([p[0].astype(bf16) for p in kp], axis=1)
    vals_t = jnp.concatenate([p[0].astype(bf16) for p in vp], axis=1)
    step(jnp.dot(q, keys_t, preferred_element_type=f32),
         lambda p: lax.dot_general(p, vals_t, nt, preferred_element_type=f32), bp_ref[0, 0])

    @pl.when(p_id == pl.num_programs(1) - 1)
    def _():
        step(lax.dot_general(q, kn_ref[0], nt, preferred_element_type=f32),
             lambda p: jnp.dot(p, vn_ref[0], preferred_element_type=f32), bn_ref[0])
        o_ref[0] = acc_ref[...] / l_ref[...]


def _attn_sample(q, k_new, v_new, qi, ki_new, wi, cache_k, cache_v, cache_kidx, page_table):
    N, tq, _ = q.shape
    n_pages = page_table.shape[1]
    past = n_pages * PAGE
    topk = min(TOPK_MAX, (past + tq) // 4)
    PP = math.gcd(PAGES_PER_STEP, n_pages)
    NP = n_pages // PP
    LP = PP * PAGE
    n_pool = cache_k.shape[0]
    KV = N_KV_HEADS * HEAD_DIM

    qi2 = qi.reshape(N, tq * IDX_HEADS, IDX_DIM)
    w2 = wi.reshape(N, tq * IDX_HEADS, 1)
    pad_rows = lambda a: jnp.pad(a, ((0, 0), (0, LANES - tq), (0, 0)))
    kin = pad_rows(ki_new.astype(bf16))

    def page_spec(shape, j):
        return pl.BlockSpec(shape, lambda n, p, pt: (pt[n, p * PP + j], 0, 0))

    sp, sn = pl.pallas_call(
        functools.partial(_sample_scores_kernel, pp=PP),
        grid_spec=pltpu.PrefetchScalarGridSpec(
            num_scalar_prefetch=1, grid=(N, NP),
            in_specs=[pl.BlockSpec((1, tq * IDX_HEADS, IDX_DIM), lambda n, p, pt: (n, 0, 0)),
                      pl.BlockSpec((1, tq * IDX_HEADS, 1), lambda n, p, pt: (n, 0, 0)),
                      pl.BlockSpec((1, LANES, IDX_DIM), lambda n, p, pt: (n, 0, 0))]
                     + [page_spec((1, IDX_DIM, PAGE), j) for j in range(PP)],
            out_specs=[pl.BlockSpec((1, 1, tq, LP), lambda n, p, pt: (n, p, 0, 0)),
                       pl.BlockSpec((1, tq, LANES), lambda n, p, pt: (n, 0, 0))]),
        out_shape=[jax.ShapeDtypeStruct((N, NP, tq, LP), f32), jax.ShapeDtypeStruct((N, tq, LANES), f32)],
        compiler_params=_cparams(("parallel", "arbitrary")),
        name="sample_scores",
    )(page_table, qi2, w2, kin, *([cache_kidx.transpose(0, 2, 1)] * PP))

    CH = KEY_CHUNK
    NCP = past // CH
    RW = N * tq
    sp2 = sp.transpose(0, 2, 1, 3).reshape(RW, NCP, CH).transpose(1, 2, 0)
    bias_t = pl.pallas_call(
        functools.partial(_sample_select_kernel, topk=topk, past=past, tq=tq),
        out_shape=jax.ShapeDtypeStruct((NCP + 1, CH, RW), f32),
        scratch_shapes=[pltpu.VMEM((NCP + 1, CH, RW), f32)],
        compiler_params=pltpu.CompilerParams(vmem_limit_bytes=VMEM_LIMIT),
        name="sample_select",
    )(sp2, sn.reshape(RW, LANES).T)
    bp = bias_t[:NCP].transpose(2, 0, 1).reshape(N, tq, NP, LP).transpose(0, 2, 1, 3)
    bn = bias_t[NCP, :LANES, :].T.reshape(N, tq, LANES)

    G = N_HEADS // N_KV_HEADS
    q5 = q.reshape(N, tq, N_KV_HEADS, G, 1, HEAD_DIM)
    eye = jnp.eye(N_KV_HEADS, dtype=q.dtype).reshape(1, 1, N_KV_HEADS, 1, N_KV_HEADS, 1)
    qbd = (q5 * eye).reshape(N, tq * N_HEADS, KV)
    kn = pad_rows(k_new.astype(bf16))
    vn = pad_rows(v_new.astype(bf16))
    ck = cache_k.transpose(0, 2, 3, 1).reshape(n_pool, KV, PAGE)
    cv = cache_v.transpose(0, 2, 3, 1).reshape(n_pool, KV, PAGE)
    R = tq * N_HEADS
    o = pl.pallas_call(
        functools.partial(_sample_attend_kernel, pp=PP),
        grid_spec=pltpu.PrefetchScalarGridSpec(
            num_scalar_prefetch=1, grid=(N, NP),
            in_specs=[pl.BlockSpec((1, R, KV), lambda n, p, pt: (n, 0, 0)),
                      pl.BlockSpec((1, 1, tq, LP), lambda n, p, pt: (n, p, 0, 0)),
                      pl.BlockSpec((1, tq, LANES), lambda n, p, pt: (n, 0, 0)),
                      pl.BlockSpec((1, LANES, KV), lambda n, p, pt: (n, 0, 0)),
                      pl.BlockSpec((1, LANES, KV), lambda n, p, pt: (n, 0, 0))]
                     + [page_spec((1, KV, PAGE), j) for j in range(PP)]
                     + [page_spec((1, KV, PAGE), j) for j in range(PP)],
            out_specs=pl.BlockSpec((1, R, KV), lambda n, p, pt: (n, 0, 0)),
            scratch_shapes=[pltpu.VMEM((R, 1), f32), pltpu.VMEM((R, 1), f32), pltpu.VMEM((R, KV), f32)]),
        out_shape=jax.ShapeDtypeStruct((N, R, KV), f32),
        compiler_params=_cparams(("parallel", "arbitrary")),
        name="sample_attend",
    )(page_table, qbd, bp, bn, kn, vn, *([ck] * PP), *([cv] * PP))
    o6 = o.reshape(N, tq, N_KV_HEADS, G, N_KV_HEADS, HEAD_DIM)
    sel = jnp.eye(N_KV_HEADS, dtype=f32).reshape(1, 1, N_KV_HEADS, 1, N_KV_HEADS, 1)
    return jnp.sum(o6 * sel, axis=4).reshape(N, tq, N_HEADS * HEAD_DIM)


HALO = 32


def _conv_kernel(u_ref, prev_ref, cw_ref, cb_ref, g_ref, b_ref, wo_ref, o_ref, buf):
    tt = u_ref.shape[1]

    @pl.when(pl.program_id(1) == 0)
    def _():
        buf[0:HALO] = prev_ref[0]

    @pl.when(pl.program_id(1) > 0)
    def _():
        buf[0:HALO] = buf[tt:tt + HALO]

    buf[HALO:HALO + tt] = u_ref[0]
    off = HALO - (D_CONV_W - 1)
    y = buf[off:off + tt] * cw_ref[0:1, :]
    for j in range(1, D_CONV_W):
        y = y + buf[off + j:off + j + tt] * cw_ref[j:j + 1, :]
    y = y + cb_ref[...]
    mu = jnp.mean(y, axis=-1, keepdims=True)
    yc = y - mu
    var = jnp.mean(yc * yc, axis=-1, keepdims=True)
    y = yc * lax.rsqrt(var + LN_EPS) * g_ref[...] + b_ref[...]
    y = y * jax.nn.sigmoid(y)
    o_ref[0] = jnp.dot(y.astype(bf16), wo_ref[...], preferred_element_type=f32)


def _conv_module(u, prev, conv_w, conv_b, ln_g, ln_b, w_conv_out_bf, tt):
    N, T, C = u.shape
    D = w_conv_out_bf.shape[1]
    prev_pad = jnp.pad(prev, ((0, 0), (HALO - prev.shape[1], 0), (0, 0)))
    cw = jnp.pad(conv_w, ((0, 32 - conv_w.shape[0]), (0, 0)))
    row2 = lambda a: a.reshape(1, -1)
    return pl.pallas_call(
        _conv_kernel,
        grid=(N, T // tt),
        in_specs=[pl.BlockSpec((1, tt, C), lambda n, t: (n, t, 0)),
                  pl.BlockSpec((1, HALO, C), lambda n, t: (n, 0, 0)),
                  pl.BlockSpec((32, C), lambda n, t: (0, 0)),
                  pl.BlockSpec((1, C), lambda n, t: (0, 0)),
                  pl.BlockSpec((1, C), lambda n, t: (0, 0)),
                  pl.BlockSpec((1, C), lambda n, t: (0, 0)),
                  pl.BlockSpec((C, D), lambda n, t: (0, 0))],
        out_specs=pl.BlockSpec((1, tt, D), lambda n, t: (n, t, 0)),
        out_shape=jax.ShapeDtypeStruct((N, T, D), f32),
        scratch_shapes=[pltpu.VMEM((HALO + tt + 8, C), f32)],
        compiler_params=_cparams(("parallel", "arbitrary")),
        name="conv_module",
    )(u, prev_pad, cw, row2(conv_b), row2(ln_g), row2(ln_b), w_conv_out_bf)


def _merge_kernel(x_ref, co_ref, ao_ref, gc_ref, ga_ref, wo_ref, g_ref, b_ref, ht_ref, htb_ref):
    m = gc_ref[...] * co_ref[...] + ga_ref[...] * ao_ref[...]
    z = ALPHA * x_ref[...] + jnp.dot(m.astype(bf16), wo_ref[...], preferred_element_type=f32)
    mu = jnp.mean(z, axis=-1, keepdims=True)
    zc = z - mu
    var = jnp.mean(zc * zc, axis=-1, keepdims=True)
    ht = (zc * lax.rsqrt(var + LN_EPS) * g_ref[...] + b_ref[...]).T
    ht_ref[...] = ht
    htb_ref[...] = ht.astype(bf16)


def _merge(x2, conv_o, attn_o, sgc, sga, w_out_bf, ln_g, ln_b, tm):
    R, D = x2.shape
    row = lambda i: (i, 0)
    cst = lambda i: (0, 0)
    col = lambda i: (0, i)
    return pl.pallas_call(
        _merge_kernel,
        grid=(R // tm,),
        in_specs=[pl.BlockSpec((tm, D), row)] * 5
                 + [pl.BlockSpec((D, D), cst), pl.BlockSpec((1, D), cst), pl.BlockSpec((1, D), cst)],
        out_specs=[pl.BlockSpec((D, tm), col), pl.BlockSpec((D, tm), col)],
        out_shape=[jax.ShapeDtypeStruct((D, R), f32), jax.ShapeDtypeStruct((D, R), bf16)],
        compiler_params=_cparams(("parallel",)),
        name="merge",
    )(x2, conv_o, attn_o, sgc, sga, w_out_bf, ln_g.reshape(1, D), ln_b.reshape(1, D))


def _extract16(s, idx, exact):
    rank = jnp.full(s.shape, float(PEER_TOPK), f32)
    tops = []
    for r in range(PEER_TOPK):
        m = jnp.max(s, axis=0, keepdims=True)
        if exact:
            pick = idx == jnp.min(jnp.where(s == m, idx, 1e9), axis=0, keepdims=True)
        else:
            pick = s == m
        rank = jnp.where(pick, float(r), rank)
        s = jnp.where(pick, -float("inf"), s)
        tops.append(m)
    return rank, jnp.concatenate(tops, axis=0)


def _route_head(s1, s2, exact):
    K, Tn = s1.shape
    T16 = float(PEER_TOPK)
    ninf = -float("inf")
    rows = lax.broadcasted_iota(jnp.int32, (K, Tn), 0).astype(f32)
    sub = lax.broadcasted_iota(jnp.int32, (8, Tn), 0)
    subf = sub.astype(f32)
    r1, t1 = _extract16(s1, rows, exact)
    r2, t2 = _extract16(s2, rows, exact)
    limits = [16, 8, 5, 4, 3, 2, 2, 2]
    cands, idxs = [], []
    cands.append(t1[0:1] + t2[0:8]); idxs.append(subf)
    cands.append(t1[0:1] + t2[8:16]); idxs.append(subf + 8.0)
    for a in range(1, 8):
        cands.append(jnp.where(sub < limits[a], t1[a:a + 1] + t2[0:8], ninf))
        idxs.append(subf + float(16 * a))
    cands.append(t1[8:16] + t2[0:1]); idxs.append((subf + 8.0) * 16.0)
    cand = jnp.concatenate(cands, axis=0)
    cidx = jnp.concatenate(idxs, axis=0)
    ecand = jnp.exp(cand - cand[0:1])
    rc, _ = _extract16(cand, cidx, exact)
    picked = jnp.where(rc < T16, 1.0, 0.0)
    z = jnp.sum(picked * ecand, axis=0, keepdims=True)
    la = [jnp.sum(picked[0:16], axis=0, keepdims=True)]
    for a in range(1, 8):
        la.append(jnp.sum(picked[8 + 8 * a:16 + 8 * a], axis=0, keepdims=True))
    ltail = picked[72:80]
    c = jnp.zeros(s1.shape, f32)
    for a in range(8):
        c = jnp.where(r1 == float(a), la[a], c)
    for a in range(8, 16):
        c = jnp.where(r1 == float(a), ltail[a - 8:a - 7], c)
    in1 = r1 < T16
    in2 = r2 < T16
    e1 = jnp.where(in1, jnp.exp(s1 - t1[0:1]), 0.0) / z
    e2 = jnp.where(in2, jnp.exp(s2 - t2[0:1]), 0.0)
    n1 = jnp.sum(jnp.where(in1, 1.0, 0.0), axis=0, keepdims=True)
    n2 = jnp.sum(jnp.where(in2, 1.0, 0.0), axis=0, keepdims=True)
    nc = jnp.sum(picked, axis=0, keepdims=True)
    ok = (n1 == T16) & (n2 == T16) & (nc == T16)
    return r2, e2, c, e1, ok


def _route_kernel(ht_ref, wq_ref, k1_ref, k2_ref, r2_ref, e2_ref, c_ref, e1_ref):
    qh = jnp.dot(wq_ref[...], ht_ref[...], preferred_element_type=f32)
    half = PEER_DQ // 2

    def store(h, r2, e2, c, e1):
        r2_ref[h] = r2.astype(bf16)
        e2_ref[h] = e2.astype(bf16)
        c_ref[h] = c
        e1_ref[h] = e1

    for h in range(PEER_HEADS):
        q1 = qh[h * PEER_DQ:h * PEER_DQ + half].astype(bf16)
        q2 = qh[h * PEER_DQ + half:(h + 1) * PEER_DQ].astype(bf16)
        s1 = jnp.dot(k1_ref[h], q1, preferred_element_type=f32)
        s2 = jnp.dot(k2_ref[h], q2, preferred_element_type=f32)
        r2, e2, c, e1, ok = _route_head(s1, s2, exact=False)
        store(h, r2, e2, c, e1)

        @pl.when(jnp.min(jnp.where(ok, 1.0, 0.0)) < 0.5)
        def _(h=h, s1=s1, s2=s2):
            store(h, *_route_head(s1, s2, exact=True)[:4])


def _route(ht_bf, wqt_bf, k1_bf, k2_bf, tn):
    D, R = ht_bf.shape
    shp = lambda dt: jax.ShapeDtypeStruct((PEER_HEADS, PEER_NKEYS, R), dt)
    ospec = pl.BlockSpec((PEER_HEADS, PEER_NKEYS, tn), lambda i: (0, 0, i))
    return pl.pallas_call(
        _route_kernel,
        grid=(R // tn,),
        in_specs=[pl.BlockSpec((D, tn), lambda i: (0, i)),
                  pl.BlockSpec(wqt_bf.shape, lambda i: (0, 0)),
                  pl.BlockSpec(k1_bf.shape, lambda i: (0, 0, 0)),
                  pl.BlockSpec(k2_bf.shape, lambda i: (0, 0, 0))],
        out_specs=[ospec] * 4,
        out_shape=[shp(bf16), shp(bf16), shp(f32), shp(f32)],
        compiler_params=_cparams(("parallel",)),
        name="peer_route",
    )(ht_bf, wqt_bf, k1_bf, k2_bf)


ET = 2048
PEER_SUBTILE = 256


def _peer_kernel(htb_ref, ht_ref, u_ref, vt_ref, r2_ref, e2_ref, c_ref, e1_ref, g_ref, b_ref,
                 y_ref, acc_ref, p_ref):
    et = pl.program_id(1)
    NK = PEER_NKEYS

    @pl.when(et == 0)
    def _():
        acc_ref[...] = jnp.zeros(acc_ref.shape, f32)

    Tn = htb_ref.shape[1]
    SL = 16
    TS = p_ref.shape[2]
    zero = jnp.zeros((), bf16)
    acts = [[jnp.dot(u_ref[ii * NK:(ii + 1) * NK, :], htb_ref[:, ts * TS:(ts + 1) * TS],
                     preferred_element_type=f32) for ii in range(ET // NK)] for ts in range(Tn // TS)]
    for ts in range(Tn // TS):
        tok = slice(ts * TS, (ts + 1) * TS)
        for ii in range(ET // NK):
            a = acts[ts][ii]
            gl = (0.5 * a * (1.0 + lax.erf(a * (2.0 ** -0.5)))).astype(bf16)
            w = [jnp.zeros((SL, TS), bf16) for _ in range(NK // SL)]
            for h in range(PEER_HEADS):
                cb = jnp.broadcast_to(c_ref[h, ii:ii + 1, tok], (SL, TS)).astype(bf16)
                eb = jnp.broadcast_to(e1_ref[h, ii:ii + 1, tok], (SL, TS)).astype(bf16)
                for s in range(NK // SL):
                    rows = slice(s * SL, (s + 1) * SL)
                    w[s] = w[s] + jnp.where(r2_ref[h, rows, tok] < cb, eb * e2_ref[h, rows, tok], zero)
            for s in range(NK // SL):
                p_ref[ts, ii * NK + s * SL:ii * NK + (s + 1) * SL, :] = w[s] * gl[s * SL:(s + 1) * SL]
        acc_ref[:, tok] += jnp.dot(vt_ref[...], p_ref[ts], preferred_element_type=f32)

    @pl.when(et == pl.num_programs(1) - 1)
    def _():
        z = ALPHA * ht_ref[...] + acc_ref[...]
        mu = jnp.mean(z, axis=0, keepdims=True)
        zc = z - mu
        var = jnp.mean(zc * zc, axis=0, keepdims=True)
        y_ref[...] = (zc * lax.rsqrt(var + LN_EPS) * g_ref[...] + b_ref[...]).T


def _peer_dense(ht_bf, ht, u_bf, vt_bf, r2, e2, c, e1, ln_g, ln_b, tn):
    D, R = ht.shape
    E = u_bf.shape[0]
    NI = ET // PEER_NKEYS
    ts = min(tn, PEER_SUBTILE)
    tok = lambda i, e: (0, i)
    rt = pl.BlockSpec((PEER_HEADS, PEER_NKEYS, tn), lambda i, e: (0, 0, i))
    ri = pl.BlockSpec((PEER_HEADS, NI, tn), lambda i, e: (0, e, i))
    return pl.pallas_call(
        _peer_kernel,
        grid=(R // tn, E // ET),
        in_specs=[pl.BlockSpec((D, tn), tok), pl.BlockSpec((D, tn), tok),
                  pl.BlockSpec((ET, D), lambda i, e: (e, 0)),
                  pl.BlockSpec((D, ET), lambda i, e: (0, e)),
                  rt, rt, ri, ri,
                  pl.BlockSpec((D, 1), lambda i, e: (0, 0)), pl.BlockSpec((D, 1), lambda i, e: (0, 0))],
        out_specs=pl.BlockSpec((tn, D), lambda i, e: (i, 0)),
        out_shape=jax.ShapeDtypeStruct((R, D), f32),
        scratch_shapes=[pltpu.VMEM((D, tn), f32), pltpu.VMEM((tn // ts, ET, ts), bf16)],
        compiler_params=_cparams(("parallel", "arbitrary")),
        name="peer_dense",
    )(ht_bf, ht, u_bf, vt_bf, r2, e2, c, e1, ln_g.reshape(D, 1), ln_b.reshape(D, 1))


def _channel_mix(ht, ht_bf, wqt_bf, k1_bf, k2_bf, u_bf, vt_bf, ln_g, ln_b, tn_route, tn_dense):
    r2, e2, c, e1 = _route(ht_bf, wqt_bf, k1_bf, k2_bf, tn_route)
    return _peer_dense(ht_bf, ht, u_bf, vt_bf, r2, e2, c, e1, ln_g, ln_b, tn_dense)


def _prep_weights(w_in, w_conv_out, w_out, peer_w_query, peer_keys1, peer_keys2, peer_u, peer_v):
    D = w_in.shape[0]
    split = 2 * (D // 2) + N_HEADS * HEAD_DIM + 2 * N_KV_HEADS * HEAD_DIM + IDX_HEADS * IDX_DIM + IDX_DIM + IDX_HEADS
    pad = (-split) % LANES
    w_pad = jnp.concatenate([w_in[:, :split], jnp.zeros((D, pad), w_in.dtype), w_in[:, split:]], axis=1).astype(bf16)
    wqt = peer_w_query.reshape(D, PEER_HEADS * PEER_DQ).T.astype(bf16)
    return dict(w_pad=w_pad, wco=w_conv_out.astype(bf16), wo=w_out.astype(bf16), wqt=wqt,
                k1=peer_keys1.astype(bf16), k2=peer_keys2.astype(bf16),
                u=peer_u.astype(bf16), vt=peer_v.T.astype(bf16))


def _group(x, pos, prev, W, conv_w, conv_b, conv_ln_g, conv_ln_b, ln1_g, ln1_b, ln2_g, ln2_b,
           attn_fn, tm, tt, tn_route, tn_dense):
    N, T, D = x.shape
    R = N * T
    x2 = x.reshape(R, D)
    tabs = _rope_tables(pos if T % tm == 0 else jnp.tile(pos, tm // T))
    u, q, k, v, qi, ki, wi, sgc, sga = _project(x2, W["w_pad"], tabs, tm)
    r3 = lambda a: a.reshape(N, T, a.shape[-1])
    attn_o = attn_fn(r3(q), r3(k), r3(v), r3(qi), r3(ki), r3(wi))
    u3 = r3(u)
    if T % tt == 0:
        conv_o = _conv_module(u3, prev, conv_w, conv_b, conv_ln_g, conv_ln_b, W["wco"], tt)
    else:
        up = jnp.pad(u3, ((0, 0), (0, tt - T), (0, 0)))
        conv_o = _conv_module(up, prev, conv_w, conv_b, conv_ln_g, conv_ln_b, W["wco"], tt)[:, :T]
    ht, ht_bf = _merge(x2, conv_o.reshape(R, D), attn_o.reshape(R, D), sgc, sga, W["wo"], ln1_g, ln1_b, tm)
    y2 = _channel_mix(ht, ht_bf, W["wqt"], W["k1"], W["k2"], W["u"], W["vt"], ln2_g, ln2_b, tn_route, tn_dense)
    return y2.reshape(N, T, D), k, v, ki, u3


def kernel(x_prompt, x_sample, cache_k, cache_v, cache_kidx, state_conv, page_table, w_in, conv_w, conv_b,
           conv_ln_g, conv_ln_b, w_conv_out, w_out, ln1_g, ln1_b, peer_w_query, peer_keys1, peer_keys2,
           peer_u, peer_v, ln2_g, ln2_b):
    W = _prep_weights(w_in, w_conv_out, w_out, peer_w_query, peer_keys1, peer_keys2, peer_u, peer_v)
    common = (W, conv_w, conv_b, conv_ln_g, conv_ln_b, ln1_g, ln1_b, ln2_g, ln2_b)
    C = conv_w.shape[1]
    keep = D_CONV_W - 1

    N, T, D = x_prompt.shape
    tm = min(256, N * T)
    yp, k_p, v_p, kidx_p, u_p = _group(
        x_prompt, jnp.arange(T, dtype=jnp.int32), jnp.zeros((N, keep, C), f32), *common,
        _attn_prompt, tm, min(512, T), min(256, N * T), min(512, N * T))
    conv_p = u_p[:, T - keep:]

    NS, tq, _ = x_sample.shape
    past = page_table.shape[1] * PAGE
    attn_s = functools.partial(_attn_sample, cache_k=cache_k, cache_v=cache_v, cache_kidx=cache_kidx,
                               page_table=page_table)
    rs = NS * tq
    ys, k_s, v_s, kidx_s, u_s = _group(
        x_sample, past + jnp.arange(tq, dtype=jnp.int32), state_conv, *common,
        lambda q, k, v, qi, ki, wi: attn_s(q, k, v, qi, ki, wi), rs, 8, rs, rs)
    conv_s = jnp.concatenate([state_conv, u_s], axis=1)[:, -keep:]

    kv4 = lambda a, n, t: a.reshape(n, t, N_KV_HEADS, HEAD_DIM)
    return (yp, ys, kv4(k_p, N, T), kv4(v_p, N, T), kidx_p.reshape(N, T, IDX_DIM), conv_p,
            kv4(k_s, NS, tq), kv4(v_s, NS, tq), kidx_s.reshape(NS, tq, IDX_DIM), conv_s)
```

```python
import functools
import math

import jax
import jax.numpy as jnp
from jax import lax
from jax.experimental import pallas as pl
from jax.experimental.pallas import tpu as pltpu

f32 = jnp.float32
bf16 = jnp.bfloat16

D_CONV_W = 31
N_HEADS = 16
N_KV_HEADS = 4
HEAD_DIM = 64
IDX_HEADS = 8
IDX_DIM = 64
TOPK_MAX = 256
ROPE_THETA = 500000.0
ROPE_HALF = 8
PAGE = 128
PEER_HEADS = 8
PEER_NKEYS = 128
PEER_DQ = 128
PEER_TOPK = 16
ALPHA = 2.0 ** 0.25
LN_EPS = 1e-5

LANES = 128
VMEM_LIMIT = 56 * 1024 * 1024
NEG = -1e30
KEY_CHUNK = 512
Q_ROWS = 256
ATTN_ROWS = 256
PAGES_PER_STEP = 16


def _cparams(sem):
    return pltpu.CompilerParams(dimension_semantics=sem, vmem_limit_bytes=VMEM_LIMIT)


def _rope_tables(pos):
    inv = jnp.power(ROPE_THETA, -jnp.arange(ROPE_HALF, dtype=f32) / ROPE_HALF)
    ang = pos.astype(f32)[:, None] * inv
    cos, sin = jnp.cos(ang), jnp.sin(ang)
    d = jnp.arange(HEAD_DIM)
    cosp = jnp.where(d < 2 * ROPE_HALF, cos[:, d % ROPE_HALF], 1.0)
    sap = jnp.where(d < ROPE_HALF, -sin[:, d % ROPE_HALF], 0.0)
    sbp = jnp.where((d >= ROPE_HALF) & (d < 2 * ROPE_HALF), sin[:, d % ROPE_HALF], 0.0)
    one = jnp.ones_like(cosp)
    zero = jnp.zeros_like(cosp)
    return (jnp.concatenate([cosp, cosp, cosp, one], axis=1),
            jnp.concatenate([sap, sap, sap, zero], axis=1),
            jnp.concatenate([sbp, sbp, sbp, zero], axis=1))


def _rope(z, cos, sa, sb):
    outs = []
    for c in range(z.shape[1] // LANES):
        zc = z[:, c * LANES:(c + 1) * LANES]
        outs.append(zc * cos + pltpu.roll(zc, LANES - ROPE_HALF, 1) * sa + pltpu.roll(zc, ROPE_HALF, 1) * sb)
    return outs[0] if len(outs) == 1 else jnp.concatenate(outs, axis=1)


def _proj_kernel(x_ref, w_ref, cos_ref, sa_ref, sb_ref,
                 u_ref, q_ref, k_ref, v_ref, qi_ref, ki_ref, wi_ref, gc_ref, ga_ref):
    x = x_ref[...].astype(bf16)
    cos, sa, sb = cos_ref[:, :LANES], sa_ref[:, :LANES], sb_ref[:, :LANES]
    cos2, sa2, sb2 = cos_ref[:, LANES:], sa_ref[:, LANES:], sb_ref[:, LANES:]

    def mm(c0, c1):
        return jnp.dot(x, w_ref[:, c0:c1], preferred_element_type=f32)

    z = mm(0, 1024)
    u_ref[...] = z[:, :512] * jax.nn.sigmoid(z[:, 512:])
    z = mm(1024, 2048)
    q_ref[...] = (_rope(z, cos, sa, sb) * (HEAD_DIM ** -0.5)).astype(bf16)
    z = mm(2048, 2560)
    k_ref[...] = _rope(z[:, :256], cos, sa, sb)
    v_ref[...] = z[:, 256:]
    z = mm(2560, 3072)
    qi_ref[...] = _rope(z, cos, sa, sb).astype(bf16)
    z = mm(3072, 3200)
    z = _rope(z, cos2, sa2, sb2)
    ki_ref[...] = z[:, :IDX_DIM]
    wi_ref[...] = z[:, IDX_DIM:IDX_DIM + IDX_HEADS]
    gc_ref[...] = jax.nn.sigmoid(mm(3200, 4224))
    ga_ref[...] = jax.nn.sigmoid(mm(4224, 5248))


def _project(x2, w_pad, tabs, tm):
    R, D = x2.shape
    rt = tabs[0].shape[0]
    nt = rt // tm
    row = lambda i: (i, 0)
    tab = lambda i: (i % nt, 0)
    widths = [(512, f32), (1024, bf16), (256, f32), (256, f32), (512, bf16), (IDX_DIM, f32), (IDX_HEADS, f32),
              (1024, f32), (1024, f32)]
    return pl.pallas_call(
        _proj_kernel,
        grid=(R // tm,),
        in_specs=[pl.BlockSpec((tm, D), row),
                  pl.BlockSpec(w_pad.shape, lambda i: (0, 0)),
                  pl.BlockSpec((tm, 256), tab), pl.BlockSpec((tm, 256), tab), pl.BlockSpec((tm, 256), tab)],
        out_specs=[pl.BlockSpec((tm, w), row) for w, _ in widths],
        out_shape=[jax.ShapeDtypeStruct((R, w), dt) for w, dt in widths],
        compiler_params=_cparams(("parallel",)),
        name="proj",
    )(x2, w_pad, *tabs)


def _select_tau_t(ST, n_ch, k, n_adm):
    _, CH, RW = ST.shape
    kf = float(k)
    inf = float("inf")
    SUB = 8

    def fold(x):
        return x.reshape(CH // SUB, SUB, RW)

    def mm_body(c, carry):
        mn, mx = carry
        s = ST[c]
        mx = jnp.maximum(mx, jnp.max(fold(s), axis=0))
        mn = jnp.minimum(mn, jnp.min(fold(jnp.where(s == -inf, inf, s)), axis=0))
        return mn, mx

    mn, mx = lax.fori_loop(0, n_ch, mm_body, (jnp.full((SUB, RW), inf, f32), jnp.full((SUB, RW), -inf, f32)))
    mn = jnp.min(mn, axis=0, keepdims=True)
    mx = jnp.max(mx, axis=0, keepdims=True)
    hi0 = mx + jnp.maximum(jnp.abs(mx), 1e-30) * (2.0 ** -20)

    def count_gt(t):
        def body(c, acc):
            return acc + jnp.sum(fold(jnp.where(ST[c] > t, 1.0, 0.0)), axis=0)

        return jnp.sum(lax.fori_loop(0, n_ch, body, jnp.zeros((SUB, RW), f32)), axis=0, keepdims=True)

    def probe(t):
        def body(c, carry):
            acc, bm = carry
            s = ST[c]
            hit = s >= t
            acc = acc + jnp.sum(fold(jnp.where(hit, 1.0, 0.0)), axis=0)
            bm = jnp.maximum(bm, jnp.max(fold(jnp.where(hit, -inf, s)), axis=0))
            return acc, bm

        acc, bm = lax.fori_loop(0, n_ch, body, (jnp.zeros((SUB, RW), f32), jnp.full((SUB, RW), -inf, f32)))
        return jnp.sum(acc, axis=0, keepdims=True), jnp.max(bm, axis=0, keepdims=True)

    def to_key(x):
        b = lax.bitcast_convert_type(x, jnp.int32)
        return b ^ ((b >> 31) & 0x7FFFFFFF)

    all_sel = n_adm <= kf
    RUN, STALL, DONE, TIED = 0.0, 1.0, 2.0, 3.0

    def cond(st):
        return jnp.logical_and(jnp.min(st[5]) < DONE, st[6] < 400)

    def body(st):
        lo, hi, hb, cl, ch, flag, it = st

        def value_half():
            h = lo + (hi - lo) * 0.5
            return h, jnp.where(h <= lo, 1.0, jnp.where(h >= hi, 1.0, 0.0))

        def key_half():
            lk, hk = to_key(lo), to_key(hi)
            midk = (lk >> 1) + (hk >> 1) + (lk & hk & 1)
            h = lax.bitcast_convert_type(midk ^ ((midk >> 31) & 0x7FFFFFFF), f32)
            return h, jnp.where(midk == lk, 1.0, 0.0)

        half, adj = lax.cond(it >= 24, key_half, value_half)
        descf = jnp.where(cl - ch <= 4.0, 1.0, jnp.where(flag == STALL, 1.0, adj))
        mid = jnp.where(descf > 0.5, hb, half)
        cnt, b = probe(mid)
        ge = cnt >= kf
        fin = jnp.where(cnt == kf, 1.0, jnp.where(ge, descf, 0.0))
        idle = jnp.where(cnt == cl, STALL, jnp.where(cnt == ch, STALL, RUN))
        nxt = jnp.where(fin > 0.5, jnp.where(cnt > kf, TIED, DONE), idle)
        running = flag < DONE
        return (jnp.where(running, jnp.where(ge, mid, lo), lo), jnp.where(ge, hi, mid), jnp.where(ge, hb, b),
                jnp.where(ge, cnt, cl), jnp.where(ge, ch, cnt), jnp.where(running, nxt, flag), it + 1)

    st = lax.while_loop(cond, body, (jnp.where(all_sel, -3e38, mn), hi0, mx, n_adm, jnp.zeros((1, RW), f32),
                                     jnp.where(all_sel, DONE, RUN), jnp.int32(0)))
    tau = st[0]
    tie = st[5] == TIED

    @pl.when(jnp.max(st[5]) > DONE)
    def _():
        need = kf - count_gt(tau)
        r = lax.broadcasted_iota(jnp.int32, (CH, CH), 0)
        c_ = lax.broadcasted_iota(jnp.int32, (CH, CH), 1)
        tri = jnp.where(c_ <= r, 1.0, 0.0).astype(bf16)

        def fix(c, run):
            s = ST[c]
            eqf = jnp.where(tie, jnp.where(s == tau, 1.0, 0.0), 0.0)
            pref = jnp.dot(tri, eqf.astype(bf16), preferred_element_type=f32) + run
            ST[c] = jnp.where(eqf * pref > need, -inf, s)
            return run + jnp.sum(eqf, axis=0, keepdims=True)

        lax.fori_loop(0, n_ch, fix, jnp.zeros((1, RW), f32))

    return tau


def _attn_prompt_kernel(qit_ref, wit_ref, ki_ref, q_ref, kt_ref, v_ref, o_ref, ST, *, topk):
    qb = pl.program_id(1)
    QB = q_ref.shape[1]
    AB = min(QB, ATTN_ROWS)
    CH = ST.shape[1]
    n_ch = (qb * QB + QB + CH - 1) // CH
    qpos = qb * QB + lax.broadcasted_iota(jnp.int32, (1, QB), 1)
    wsc = wit_ref[0] * ((IDX_HEADS * IDX_DIM) ** -0.5)
    subs = [slice(r0, r0 + AB) for r0 in range(0, QB, AB)]
    qit = jnp.concatenate([qit_ref[0, h] for h in range(IDX_HEADS)], axis=1)

    def score_chunk(c, carry):
        s = jnp.dot(ki_ref[0, c], qit, preferred_element_type=f32)
        sc = jnp.maximum(s[:, 0:QB], 0.0) * wsc[0:1]
        for h in range(1, IDX_HEADS):
            sc = sc + jnp.maximum(s[:, h * QB:(h + 1) * QB], 0.0) * wsc[h:h + 1]
        kpos = c * CH + lax.broadcasted_iota(jnp.int32, (CH, 1), 0)
        ST[c] = jnp.where(kpos <= qpos, sc, -float("inf"))
        return carry

    lax.fori_loop(0, n_ch, score_chunk, 0)
    tau = _select_tau_t(ST, n_ch, topk, (qpos + 1).astype(f32))

    G = N_HEADS // N_KV_HEADS
    for si, rs in enumerate(subs):
        taub = tau[:, rs]
        n_ch_s = (qb * QB + (si + 1) * AB + CH - 1) // CH
        qgs = [jnp.concatenate([q_ref[0, rs, (g * G + j) * HEAD_DIM:(g * G + j + 1) * HEAD_DIM]
                                for j in range(G)], axis=0) for g in range(N_KV_HEADS)]

        def chunk(c, carry, rs=rs, taub=taub, qgs=qgs):
            bias = jnp.where(ST[c, :, rs] >= taub, 0.0, NEG).T[None]
            off = pl.multiple_of(c * CH, CH)
            out = []
            for g in range(N_KV_HEADS):
                m, acc = carry[g]
                s = jnp.dot(qgs[g], kt_ref[0, c, g], preferred_element_type=f32)
                s = (s.reshape(G, AB, CH) + bias).reshape(G * AB, CH)
                m_new = jnp.maximum(m, jnp.max(s, axis=1, keepdims=True))
                p = jnp.exp(s - m_new).astype(bf16)
                acc = jnp.exp(m - m_new) * acc + jnp.dot(p, v_ref[0, g, pl.ds(off, CH), :],
                                                          preferred_element_type=f32)
                out.append((m_new, acc))
            return tuple(out)

        init = tuple((jnp.full((G * AB, 1), NEG, f32), jnp.zeros((G * AB, LANES), f32))
                     for _ in range(N_KV_HEADS))
        res = lax.fori_loop(0, n_ch_s, chunk, init)
        for g in range(N_KV_HEADS):
            acc = res[g][1]
            o = acc[:, :HEAD_DIM] / acc[:, HEAD_DIM:HEAD_DIM + 1]
            for j in range(G):
                h = g * G + j
                o_ref[0, rs, h * HEAD_DIM:(h + 1) * HEAD_DIM] = o[j * AB:(j + 1) * AB]


def _attn_prompt(q, k, v, qi, ki, wi):
    N, T, D = q.shape
    QB = min(Q_ROWS, T)
    topk = min(TOPK_MAX, T // 4)
    CH = min(KEY_CHUNK, T)
    NC = T // CH
    kic = ki.astype(bf16).reshape(N, NC, CH, IDX_DIM)
    qit = qi.reshape(N, T, IDX_HEADS, IDX_DIM).transpose(0, 2, 3, 1)
    wit = wi.transpose(0, 2, 1)
    kt = k.astype(bf16).reshape(N, NC, CH, N_KV_HEADS, HEAD_DIM).transpose(0, 1, 3, 4, 2)
    vh = v.astype(bf16).reshape(N, T, N_KV_HEADS, HEAD_DIM).transpose(0, 2, 1, 3)
    vh = jnp.concatenate([vh, jnp.ones(vh.shape[:3] + (1,), bf16),
                          jnp.zeros(vh.shape[:3] + (LANES - HEAD_DIM - 1,), bf16)], axis=3)
    return pl.pallas_call(
        functools.partial(_attn_prompt_kernel, topk=topk),
        grid=(N, T // QB),
        in_specs=[pl.BlockSpec((1, IDX_HEADS, IDX_DIM, QB), lambda n, b: (n, 0, 0, b)),
                  pl.BlockSpec((1, IDX_HEADS, QB), lambda n, b: (n, 0, b)),
                  pl.BlockSpec((1, NC, CH, IDX_DIM), lambda n, b: (n, 0, 0, 0)),
                  pl.BlockSpec((1, QB, D), lambda n, b: (n, b, 0)),
                  pl.BlockSpec((1, NC, N_KV_HEADS, HEAD_DIM, CH), lambda n, b: (n, 0, 0, 0, 0)),
                  pl.BlockSpec((1, N_KV_HEADS, T, LANES), lambda n, b: (n, 0, 0, 0))],
        out_specs=pl.BlockSpec((1, QB, D), lambda n, b: (n, b, 0)),
        out_shape=jax.ShapeDtypeStruct((N, T, D), f32),
        scratch_shapes=[pltpu.VMEM((NC, CH, QB), f32)],
        compiler_params=_cparams(("parallel", "arbitrary")),
        name="attn_prompt",
    )(qit, wit, kic, q, kt, vh)


def _sample_scores_kernel(pt_ref, qi_ref, w_ref, kin_ref, *rest, pp):
    pages = rest[:pp]
    sp_ref, sn_ref = rest[pp:]
    TQ = sp_ref.shape[2]
    qi = qi_ref[0]
    w = w_ref[0] * ((IDX_HEADS * IDX_DIM) ** -0.5)
    nt = (((1,), (1,)), ((), ()))

    def scores(s):
        s = jnp.maximum(s, 0.0) * w
        return jnp.sum(s.reshape(TQ, IDX_HEADS, s.shape[1]), axis=1)

    keys_t = jnp.concatenate([p[0].astype(bf16) for p in pages], axis=1)
    sp_ref[0, 0] = scores(jnp.dot(qi, keys_t, preferred_element_type=f32))

    @pl.when(pl.program_id(1) == 0)
    def _():
        s = scores(lax.dot_general(qi, kin_ref[0], nt, preferred_element_type=f32))
        t = lax.broadcasted_iota(jnp.int32, s.shape, 0)
        j = lax.broadcasted_iota(jnp.int32, s.shape, 1)
        sn_ref[0] = jnp.where(j <= t, s, -float("inf"))


def _sample_select_kernel(sp_ref, sn_ref, bias_ref, ST, *, topk, past, tq):
    NCP = sp_ref.shape[0]
    CH, RW = ST.shape[1], ST.shape[2]
    for c in range(NCP):
        ST[c] = sp_ref[c]
    ST[NCP] = jnp.concatenate([sn_ref[...], jnp.full((CH - LANES, RW), -float("inf"), f32)], axis=0)
    t = lax.broadcasted_iota(jnp.int32, (1, RW), 1) % tq
    tau = _select_tau_t(ST, NCP + 1, topk, (past + 1 + t).astype(f32))
    for c in range(NCP + 1):
        bias_ref[c] = jnp.where(ST[c] >= tau, 0.0, NEG)


def _sample_attend_kernel(pt_ref, q_ref, bp_ref, bn_ref, kn_ref, vn_ref, *rest, pp):
    kp = rest[:pp]
    vp = rest[pp:2 * pp]
    o_ref, m_ref, l_ref, acc_ref = rest[2 * pp:]
    p_id = pl.program_id(1)
    q = q_ref[0]
    R = q.shape[0]
    TQ = bp_ref.shape[2]
    nt = (((1,), (1,)), ((), ()))

    @pl.when(p_id == 0)
    def _():
        m_ref[...] = jnp.full(m_ref.shape, NEG, f32)
        l_ref[...] = jnp.zeros(l_ref.shape, f32)
        acc_ref[...] = jnp.zeros(acc_ref.shape, f32)

    def step(s, pv, bias):
        L = s.shape[1]
        b = jnp.broadcast_to(bias[:, None, :], (TQ, R // TQ, L)).reshape(R, L)
        s = jnp.where(b < 0.0, NEG, s)
        m = m_ref[...]
        m_new = jnp.maximum(m, jnp.max(s, axis=1, keepdims=True))
        p = jnp.exp(s - m_new)
        a = jnp.exp(m - m_new)
        l_ref[...] = a * l_ref[...] + jnp.sum(p, axis=1, keepdims=True)
        acc_ref[...] = a * acc_ref[...] + pv(p.astype(bf16))
        m_ref[...] = m_new

    keys_t = jnp.concatenate([p[0].astype(bf16) for p in kp], axis=1)
    vals_t = jnp.concatenate([p[0].astype(bf16) for p in vp], axis=1)
    step(jnp.dot(q, keys_t, preferred_element_type=f32),
         lambda p: lax.dot_general(p, vals_t, nt, preferred_element_type=f32), bp_ref[0, 0])

    @pl.when(p_id == pl.num_programs(1) - 1)
    def _():
        step(lax.dot_general(q, kn_ref[0], nt, preferred_element_type=f32),
             lambda p: jnp.dot(p, vn_ref[0], preferred_element_type=f32), bn_ref[0])
        o_ref[0] = acc_ref[...] / l_ref[...]


def _attn_sample(q, k_new, v_new, qi, ki_new, wi, cache_k, cache_v, cache_kidx, page_table):
    N, tq, _ = q.shape
    n_pages = page_table.shape[1]
    past = n_pages * PAGE
    topk = min(TOPK_MAX, (past + tq) // 4)
    PP = math.gcd(PAGES_PER_STEP, n_pages)
    NP = n_pages // PP
    LP = PP * PAGE
    n_pool = cache_k.shape[0]
    KV = N_KV_HEADS * HEAD_DIM

    qi2 = qi.reshape(N, tq * IDX_HEADS, IDX_DIM)
    w2 = wi.reshape(N, tq * IDX_HEADS, 1)
    pad_rows = lambda a: jnp.pad(a, ((0, 0), (0, LANES - tq), (0, 0)))
    kin = pad_rows(ki_new.astype(bf16))

    def page_spec(shape, j):
        return pl.BlockSpec(shape, lambda n, p, pt: (pt[n, p * PP + j], 0, 0))

    sp, sn = pl.pallas_call(
        functools.partial(_sample_scores_kernel, pp=PP),
        grid_spec=pltpu.PrefetchScalarGridSpec(
            num_scalar_prefetch=1, grid=(N, NP),
            in_specs=[pl.BlockSpec((1, tq * IDX_HEADS, IDX_DIM), lambda n, p, pt: (n, 0, 0)),
                      pl.BlockSpec((1, tq * IDX_HEADS, 1), lambda n, p, pt: (n, 0, 0)),
                      pl.BlockSpec((1, LANES, IDX_DIM), lambda n, p, pt: (n, 0, 0))]
                     + [page_spec((1, IDX_DIM, PAGE), j) for j in range(PP)],
            out_specs=[pl.BlockSpec((1, 1, tq, LP), lambda n, p, pt: (n, p, 0, 0)),
                       pl.BlockSpec((1, tq, LANES), lambda n, p, pt: (n, 0, 0))]),
        out_shape=[jax.ShapeDtypeStruct((N, NP, tq, LP), f32), jax.ShapeDtypeStruct((N, tq, LANES), f32)],
        compiler_params=_cparams(("parallel", "arbitrary")),
        name="sample_scores",
    )(page_table, qi2, w2, kin, *([cache_kidx.transpose(0, 2, 1)] * PP))

    CH = KEY_CHUNK
    NCP = past // CH
    RW = N * tq
    sp2 = sp.transpose(0, 2, 1, 3).reshape(RW, NCP, CH).transpose(1, 2, 0)
    bias_t = pl.pallas_call(
        functools.partial(_sample_select_kernel, topk=topk, past=past, tq=tq),
        out_shape=jax.ShapeDtypeStruct((NCP + 1, CH, RW), f32),
        scratch_shapes=[pltpu.VMEM((NCP + 1, CH, RW), f32)],
        compiler_params=pltpu.CompilerParams(vmem_limit_bytes=VMEM_LIMIT),
        name="sample_select",
    )(sp2, sn.reshape(RW, LANES).T)
    bp = bias_t[:NCP].transpose(2, 0, 1).reshape(N, tq, NP, LP).transpose(0, 2, 1, 3)
    bn = bias_t[NCP, :LANES, :].T.reshape(N, tq, LANES)

    G = N_HEADS // N_KV_HEADS
    q5 = q.reshape(N, tq, N_KV_HEADS, G, 1, HEAD_DIM)
    eye = jnp.eye(N_KV_HEADS, dtype=q.dtype).reshape(1, 1, N_KV_HEADS, 1, N_KV_HEADS, 1)
    qbd = (q5 * eye).reshape(N, tq * N_HEADS, KV)
    kn = pad_rows(k_new.astype(bf16))
    vn = pad_rows(v_new.astype(bf16))
    ck = cache_k.transpose(0, 2, 3, 1).reshape(n_pool, KV, PAGE)
    cv = cache_v.transpose(0, 2, 3, 1).reshape(n_pool, KV, PAGE)
    R = tq * N_HEADS
    o = pl.pallas_call(
        functools.partial(_sample_attend_kernel, pp=PP),
        grid_spec=pltpu.PrefetchScalarGridSpec(
            num_scalar_prefetch=1, grid=(N, NP),
            in_specs=[pl.BlockSpec((1, R, KV), lambda n, p, pt: (n, 0, 0)),
                      pl.BlockSpec((1, 1, tq, LP), lambda n, p, pt: (n, p, 0, 0)),
                      pl.BlockSpec((1, tq, LANES), lambda n, p, pt: (n, 0, 0)),
                      pl.BlockSpec((1, LANES, KV), lambda n, p, pt: (n, 0, 0)),
                      pl.BlockSpec((1, LANES, KV), lambda n, p, pt: (n, 0, 0))]
                     + [page_spec((1, KV, PAGE), j) for j in range(PP)]
                     + [page_spec((1, KV, PAGE), j) for j in range(PP)],
            out_specs=pl.BlockSpec((1, R, KV), lambda n, p, pt: (n, 0, 0)),
            scratch_shapes=[pltpu.VMEM((R, 1), f32), pltpu.VMEM((R, 1), f32), pltpu.VMEM((R, KV), f32)]),
        out_shape=jax.ShapeDtypeStruct((N, R, KV), f32),
        compiler_params=_cparams(("parallel", "arbitrary")),
        name="sample_attend",
    )(page_table, qbd, bp, bn, kn, vn, *([ck] * PP), *([cv] * PP))
    o6 = o.reshape(N, tq, N_KV_HEADS, G, N_KV_HEADS, HEAD_DIM)
    sel = jnp.eye(N_KV_HEADS, dtype=f32).reshape(1, 1, N_KV_HEADS, 1, N_KV_HEADS, 1)
    return jnp.sum(o6 * sel, axis=4).reshape(N, tq, N_HEADS * HEAD_DIM)


HALO = 32


def _conv_kernel(u_ref, prev_ref, cw_ref, cb_ref, g_ref, b_ref, wo_ref, o_ref, buf):
    tt = u_ref.shape[1]

    @pl.when(pl.program_id(1) == 0)
    def _():
        buf[0:HALO] = prev_ref[0]

    @pl.when(pl.program_id(1) > 0)
    def _():
        buf[0:HALO] = buf[tt:tt + HALO]

    buf[HALO:HALO + tt] = u_ref[0]
    off = HALO - (D_CONV_W - 1)
    y = buf[off:off + tt] * cw_ref[0:1, :]
    for j in range(1, D_CONV_W):
        y = y + buf[off + j:off + j + tt] * cw_ref[j:j + 1, :]
    y = y + cb_ref[...]
    mu = jnp.mean(y, axis=-1, keepdims=True)
    yc = y - mu
    var = jnp.mean(yc * yc, axis=-1, keepdims=True)
    y = yc * lax.rsqrt(var + LN_EPS) * g_ref[...] + b_ref[...]
    y = y * jax.nn.sigmoid(y)
    o_ref[0] = jnp.dot(y.astype(bf16), wo_ref[...], preferred_element_type=f32)


def _conv_module(u, prev, conv_w, conv_b, ln_g, ln_b, w_conv_out_bf, tt):
    N, T, C = u.shape
    D = w_conv_out_bf.shape[1]
    prev_pad = jnp.pad(prev, ((0, 0), (HALO - prev.shape[1], 0), (0, 0)))
    cw = jnp.pad(conv_w, ((0, 32 - conv_w.shape[0]), (0, 0)))
    row2 = lambda a: a.reshape(1, -1)
    return pl.pallas_call(
        _conv_kernel,
        grid=(N, T // tt),
        in_specs=[pl.BlockSpec((1, tt, C), lambda n, t: (n, t, 0)),
                  pl.BlockSpec((1, HALO, C), lambda n, t: (n, 0, 0)),
                  pl.BlockSpec((32, C), lambda n, t: (0, 0)),
                  pl.BlockSpec((1, C), lambda n, t: (0, 0)),
                  pl.BlockSpec((1, C), lambda n, t: (0, 0)),
                  pl.BlockSpec((1, C), lambda n, t: (0, 0)),
                  pl.BlockSpec((C, D), lambda n, t: (0, 0))],
        out_specs=pl.BlockSpec((1, tt, D), lambda n, t: (n, t, 0)),
        out_shape=jax.ShapeDtypeStruct((N, T, D), f32),
        scratch_shapes=[pltpu.VMEM((HALO + tt + 8, C), f32)],
        compiler_params=_cparams(("parallel", "arbitrary")),
        name="conv_module",
    )(u, prev_pad, cw, row2(conv_b), row2(ln_g), row2(ln_b), w_conv_out_bf)


def _merge_kernel(x_ref, co_ref, ao_ref, gc_ref, ga_ref, wo_ref, g_ref, b_ref, ht_ref, htb_ref):
    m = gc_ref[...] * co_ref[...] + ga_ref[...] * ao_ref[...]
    z = ALPHA * x_ref[...] + jnp.dot(m.astype(bf16), wo_ref[...], preferred_element_type=f32)
    mu = jnp.mean(z, axis=-1, keepdims=True)
    zc = z - mu
    var = jnp.mean(zc * zc, axis=-1, keepdims=True)
    ht = (zc * lax.rsqrt(var + LN_EPS) * g_ref[...] + b_ref[...]).T
    ht_ref[...] = ht
    htb_ref[...] = ht.astype(bf16)


def _merge(x2, conv_o, attn_o, sgc, sga, w_out_bf, ln_g, ln_b, tm):
    R, D = x2.shape
    row = lambda i: (i, 0)
    cst = lambda i: (0, 0)
    col = lambda i: (0, i)
    return pl.pallas_call(
        _merge_kernel,
        grid=(R // tm,),
        in_specs=[pl.BlockSpec((tm, D), row)] * 5
                 + [pl.BlockSpec((D, D), cst), pl.BlockSpec((1, D), cst), pl.BlockSpec((1, D), cst)],
        out_specs=[pl.BlockSpec((D, tm), col), pl.BlockSpec((D, tm), col)],
        out_shape=[jax.ShapeDtypeStruct((D, R), f32), jax.ShapeDtypeStruct((D, R), bf16)],
        compiler_params=_cparams(("parallel",)),
        name="merge",
    )(x2, conv_o, attn_o, sgc, sga, w_out_bf, ln_g.reshape(1, D), ln_b.reshape(1, D))


def _extract16(s, idx, exact, want_rank=True):
    rank = jnp.full(s.shape, float(PEER_TOPK), f32) if want_rank else None
    tops = []
    for r in range(PEER_TOPK):
        m = jnp.max(s, axis=0, keepdims=True)
        if exact:
            pick = idx == jnp.min(jnp.where(s == m, idx, 1e9), axis=0, keepdims=True)
        else:
            pick = s == m
        if want_rank:
            rank = jnp.where(pick, float(r), rank)
        s = jnp.where(pick, -float("inf"), s)
        tops.append(m)
    return rank, jnp.concatenate(tops, axis=0)


def _route_head(s1, s2, exact):
    K, Tn = s1.shape
    T16 = float(PEER_TOPK)
    ninf = -float("inf")
    rows = lax.broadcasted_iota(jnp.int32, (K, Tn), 0).astype(f32)
    sub = lax.broadcasted_iota(jnp.int32, (8, Tn), 0)
    subf = sub.astype(f32)
    r1, t1 = _extract16(s1, rows, exact, want_rank=exact)
    r2, t2 = _extract16(s2, rows, exact)
    is_a = (lambda a: r1 == float(a)) if exact else (lambda a: s1 == t1[a:a + 1])
    limits = [16, 8, 5, 4, 3, 2, 2, 2]
    cands, idxs = [], []
    cands.append(t1[0:1] + t2[0:8]); idxs.append(subf)
    cands.append(t1[0:1] + t2[8:16]); idxs.append(subf + 8.0)
    for a in range(1, 8):
        cands.append(jnp.where(sub < limits[a], t1[a:a + 1] + t2[0:8], ninf))
        idxs.append(subf + float(16 * a))
    cands.append(t1[8:16] + t2[0:1]); idxs.append((subf + 8.0) * 16.0)
    cand = jnp.concatenate(cands, axis=0)
    cidx = jnp.concatenate(idxs, axis=0)
    ecand = jnp.exp(cand - cand[0:1])
    rc, _ = _extract16(cand, cidx, exact)
    picked = jnp.where(rc < T16, 1.0, 0.0)
    z = jnp.sum(picked * ecand, axis=0, keepdims=True)
    la = [jnp.sum(picked[0:16], axis=0, keepdims=True)]
    for a in range(1, 8):
        la.append(jnp.sum(picked[8 + 8 * a:16 + 8 * a], axis=0, keepdims=True))
    ltail = picked[72:80]
    c = jnp.zeros(s1.shape, f32)
    for a in range(8):
        c = jnp.where(is_a(a), la[a], c)
    for a in range(8, 16):
        c = jnp.where(is_a(a), ltail[a - 8:a - 7], c)
    in1 = (r1 < T16) if exact else (s1 >= t1[PEER_TOPK - 1:PEER_TOPK])
    in2 = r2 < T16
    e1 = jnp.where(in1, jnp.exp(s1 - t1[0:1]), 0.0) / z
    e2 = jnp.where(in2, jnp.exp(s2 - t2[0:1]), 0.0)
    n1 = jnp.sum(jnp.where(in1, 1.0, 0.0), axis=0, keepdims=True)
    n2 = jnp.sum(jnp.where(in2, 1.0, 0.0), axis=0, keepdims=True)
    nc = jnp.sum(picked, axis=0, keepdims=True)
    ok = (n1 == T16) & (n2 == T16) & (nc == T16)
    return r2, e2, c, e1, ok


def _route_kernel(ht_ref, wq_ref, k1_ref, k2_ref, r2_ref, e2_ref, c_ref, e1_ref):
    qh = jnp.dot(wq_ref[...], ht_ref[...], preferred_element_type=f32)
    half = PEER_DQ // 2

    def store(h, r2, e2, c, e1):
        r2_ref[h] = r2.astype(bf16)
        e2_ref[h] = e2.astype(bf16)
        c_ref[h] = c
        e1_ref[h] = e1

    for h in range(PEER_HEADS):
        q1 = qh[h * PEER_DQ:h * PEER_DQ + half].astype(bf16)
        q2 = qh[h * PEER_DQ + half:(h + 1) * PEER_DQ].astype(bf16)
        s1 = jnp.dot(k1_ref[h], q1, preferred_element_type=f32)
        s2 = jnp.dot(k2_ref[h], q2, preferred_element_type=f32)
        r2, e2, c, e1, ok = _route_head(s1, s2, exact=False)
        store(h, r2, e2, c, e1)

        @pl.when(jnp.min(jnp.where(ok, 1.0, 0.0)) < 0.5)
        def _(h=h, s1=s1, s2=s2):
            store(h, *_route_head(s1, s2, exact=True)[:4])


def _route(ht_bf, wqt_bf, k1_bf, k2_bf, tn):
    D, R = ht_bf.shape
    shp = lambda dt: jax.ShapeDtypeStruct((PEER_HEADS, PEER_NKEYS, R), dt)
    ospec = pl.BlockSpec((PEER_HEADS, PEER_NKEYS, tn), lambda i: (0, 0, i))
    return pl.pallas_call(
        _route_kernel,
        grid=(R // tn,),
        in_specs=[pl.BlockSpec((D, tn), lambda i: (0, i)),
                  pl.BlockSpec(wqt_bf.shape, lambda i: (0, 0)),
                  pl.BlockSpec(k1_bf.shape, lambda i: (0, 0, 0)),
                  pl.BlockSpec(k2_bf.shape, lambda i: (0, 0, 0))],
        out_specs=[ospec] * 4,
        out_shape=[shp(bf16), shp(bf16), shp(f32), shp(f32)],
        compiler_params=_cparams(("parallel",)),
        name="peer_route",
    )(ht_bf, wqt_bf, k1_bf, k2_bf)


ET = 2048
PEER_SUBTILE = 256


def _peer_kernel(htb_ref, ht_ref, u_ref, vt_ref, r2_ref, e2_ref, c_ref, e1_ref, g_ref, b_ref,
                 y_ref, acc_ref, p_ref):
    et = pl.program_id(1)
    NK = PEER_NKEYS

    @pl.when(et == 0)
    def _():
        acc_ref[...] = jnp.zeros(acc_ref.shape, f32)

    Tn = htb_ref.shape[1]
    SL = 16
    TS = p_ref.shape[2]
    zero = jnp.zeros((), bf16)
    acts = [[jnp.dot(u_ref[ii * NK:(ii + 1) * NK, :], htb_ref[:, ts * TS:(ts + 1) * TS],
                     preferred_element_type=f32) for ii in range(ET // NK)] for ts in range(Tn // TS)]
    for ts in range(Tn // TS):
        tok = slice(ts * TS, (ts + 1) * TS)
        for ii in range(ET // NK):
            a = acts[ts][ii]
            gl = (0.5 * a * (1.0 + lax.erf(a * (2.0 ** -0.5)))).astype(bf16)
            w = [jnp.zeros((SL, TS), bf16) for _ in range(NK // SL)]
            for h in range(PEER_HEADS):
                cb = jnp.broadcast_to(c_ref[h, ii:ii + 1, tok], (SL, TS)).astype(bf16)
                eb = jnp.broadcast_to(e1_ref[h, ii:ii + 1, tok], (SL, TS)).astype(bf16)
                for s in range(NK // SL):
                    rows = slice(s * SL, (s + 1) * SL)
                    w[s] = w[s] + jnp.where(r2_ref[h, rows, tok] < cb, eb * e2_ref[h, rows, tok], zero)
            for s in range(NK // SL):
                p_ref[ts, ii * NK + s * SL:ii * NK + (s + 1) * SL, :] = w[s] * gl[s * SL:(s + 1) * SL]
        acc_ref[:, tok] += jnp.dot(vt_ref[...], p_ref[ts], preferred_element_type=f32)

    @pl.when(et == pl.num_programs(1) - 1)
    def _():
        z = ALPHA * ht_ref[...] + acc_ref[...]
        mu = jnp.mean(z, axis=0, keepdims=True)
        zc = z - mu
        var = jnp.mean(zc * zc, axis=0, keepdims=True)
        y_ref[...] = (zc * lax.rsqrt(var + LN_EPS) * g_ref[...] + b_ref[...]).T


def _peer_dense(ht_bf, ht, u_bf, vt_bf, r2, e2, c, e1, ln_g, ln_b, tn):
    D, R = ht.shape
    E = u_bf.shape[0]
    NI = ET // PEER_NKEYS
    ts = min(tn, PEER_SUBTILE)
    tok = lambda i, e: (0, i)
    rt = pl.BlockSpec((PEER_HEADS, PEER_NKEYS, tn), lambda i, e: (0, 0, i))
    ri = pl.BlockSpec((PEER_HEADS, NI, tn), lambda i, e: (0, e, i))
    return pl.pallas_call(
        _peer_kernel,
        grid=(R // tn, E // ET),
        in_specs=[pl.BlockSpec((D, tn), tok), pl.BlockSpec((D, tn), tok),
                  pl.BlockSpec((ET, D), lambda i, e: (e, 0)),
                  pl.BlockSpec((D, ET), lambda i, e: (0, e)),
                  rt, rt, ri, ri,
                  pl.BlockSpec((D, 1), lambda i, e: (0, 0)), pl.BlockSpec((D, 1), lambda i, e: (0, 0))],
        out_specs=pl.BlockSpec((tn, D), lambda i, e: (i, 0)),
        out_shape=jax.ShapeDtypeStruct((R, D), f32),
        scratch_shapes=[pltpu.VMEM((D, tn), f32), pltpu.VMEM((tn // ts, ET, ts), bf16)],
        compiler_params=_cparams(("parallel", "arbitrary")),
        name="peer_dense",
    )(ht_bf, ht, u_bf, vt_bf, r2, e2, c, e1, ln_g.reshape(D, 1), ln_b.reshape(D, 1))


def _channel_mix(ht, ht_bf, wqt_bf, k1_bf, k2_bf, u_bf, vt_bf, ln_g, ln_b, tn_route, tn_dense):
    r2, e2, c, e1 = _route(ht_bf, wqt_bf, k1_bf, k2_bf, tn_route)
    return _peer_dense(ht_bf, ht, u_bf, vt_bf, r2, e2, c, e1, ln_g, ln_b, tn_dense)


def _prep_weights(w_in, w_conv_out, w_out, peer_w_query, peer_keys1, peer_keys2, peer_u, peer_v):
    D = w_in.shape[0]
    split = 2 * (D // 2) + N_HEADS * HEAD_DIM + 2 * N_KV_HEADS * HEAD_DIM + IDX_HEADS * IDX_DIM + IDX_DIM + IDX_HEADS
    pad = (-split) % LANES
    w_pad = jnp.concatenate([w_in[:, :split], jnp.zeros((D, pad), w_in.dtype), w_in[:, split:]], axis=1).astype(bf16)
    wqt = peer_w_query.reshape(D, PEER_HEADS * PEER_DQ).T.astype(bf16)
    return dict(w_pad=w_pad, wco=w_conv_out.astype(bf16), wo=w_out.astype(bf16), wqt=wqt,
                k1=peer_keys1.astype(bf16), k2=peer_keys2.astype(bf16),
                u=peer_u.astype(bf16), vt=peer_v.T.astype(bf16))


def _group(x, pos, prev, W, conv_w, conv_b, conv_ln_g, conv_ln_b, ln1_g, ln1_b, ln2_g, ln2_b,
           attn_fn, tm, tt, tn_route, tn_dense):
    N, T, D = x.shape
    R = N * T
    x2 = x.reshape(R, D)
    tabs = _rope_tables(pos if T % tm == 0 else jnp.tile(pos, tm // T))
    u, q, k, v, qi, ki, wi, sgc, sga = _project(x2, W["w_pad"], tabs, tm)
    r3 = lambda a: a.reshape(N, T, a.shape[-1])
    attn_o = attn_fn(r3(q), r3(k), r3(v), r3(qi), r3(ki), r3(wi))
    u3 = r3(u)
    if T % tt == 0:
        conv_o = _conv_module(u3, prev, conv_w, conv_b, conv_ln_g, conv_ln_b, W["wco"], tt)
    else:
        up = jnp.pad(u3, ((0, 0), (0, tt - T), (0, 0)))
        conv_o = _conv_module(up, prev, conv_w, conv_b, conv_ln_g, conv_ln_b, W["wco"], tt)[:, :T]
    ht, ht_bf = _merge(x2, conv_o.reshape(R, D), attn_o.reshape(R, D), sgc, sga, W["wo"], ln1_g, ln1_b, tm)
    y2 = _channel_mix(ht, ht_bf, W["wqt"], W["k1"], W["k2"], W["u"], W["vt"], ln2_g, ln2_b, tn_route, tn_dense)
    return y2.reshape(N, T, D), k, v, ki, u3


def kernel(x_prompt, x_sample, cache_k, cache_v, cache_kidx, state_conv, page_table, w_in, conv_w, conv_b,
           conv_ln_g, conv_ln_b, w_conv_out, w_out, ln1_g, ln1_b, peer_w_query, peer_keys1, peer_keys2,
           peer_u, peer_v, ln2_g, ln2_b):
    W = _prep_weights(w_in, w_conv_out, w_out, peer_w_query, peer_keys1, peer_keys2, peer_u, peer_v)
    common = (W, conv_w, conv_b, conv_ln_g, conv_ln_b, ln1_g, ln1_b, ln2_g, ln2_b)
    C = conv_w.shape[1]
    keep = D_CONV_W - 1

    N, T, D = x_prompt.shape
    tm = min(256, N * T)
    yp, k_p, v_p, kidx_p, u_p = _group(
        x_prompt, jnp.arange(T, dtype=jnp.int32), jnp.zeros((N, keep, C), f32), *common,
        _attn_prompt, tm, min(512, T), min(256, N * T), min(512, N * T))
    conv_p = u_p[:, T - keep:]

    NS, tq, _ = x_sample.shape
    past = page_table.shape[1] * PAGE
    attn_s = functools.partial(_attn_sample, cache_k=cache_k, cache_v=cache_v, cache_kidx=cache_kidx,
                               page_table=page_table)
    rs = NS * tq
    ys, k_s, v_s, kidx_s, u_s = _group(
        x_sample, past + jnp.arange(tq, dtype=jnp.int32), state_conv, *common,
        lambda q, k, v, qi, ki, wi: attn_s(q, k, v, qi, ki, wi), rs, 8, rs, rs)
    conv_s = jnp.concatenate([state_conv, u_s], axis=1)[:, -keep:]

    kv4 = lambda a, n, t: a.reshape(n, t, N_KV_HEADS, HEAD_DIM)
    return (yp, ys, kv4(k_p, N, T), kv4(v_p, N, T), kidx_p.reshape(N, T, IDX_DIM), conv_p,
            kv4(k_s, NS, tq), kv4(v_s, NS, tq), kidx_s.reshape(NS, tq, IDX_DIM), conv_s)
```
